```python
import jax, jax.numpy as jnp
from jax import lax
import numpy as np

D_MODEL = 2048
BATCH = 4
SEQ = 4096
DEPTH = 1

CHUNK = 64
SWA_Q_HEADS = 16
SWA_KV_HEADS = 4
SWA_HEAD_DIM = 64
SWA_GROUP = SWA_Q_HEADS // SWA_KV_HEADS
WINDOW = 128
WINDOW_CHUNKS = WINDOW // CHUNK
BAND = (WINDOW_CHUNKS + 1) * CHUNK
SWA_Q_W = SWA_Q_HEADS * SWA_HEAD_DIM
SWA_KV_W = SWA_KV_HEADS * SWA_HEAD_DIM
GDN_HEADS = 8
GDN_HEAD_DIM = 128
GDN_W = GDN_HEADS * GDN_HEAD_DIM
CONV_WIDTH = 4
IN_SIZES = (SWA_Q_W, SWA_KV_W, SWA_KV_W, GDN_W, GDN_W, GDN_W, GDN_W,
            GDN_HEADS, GDN_HEADS, D_MODEL, D_MODEL)
IN_WIDTH = sum(IN_SIZES)
N_GROUPS = 4
EXPERTS_PER_GROUP = 8
N_EXPERTS = N_GROUPS * EXPERTS_PER_GROUP
TOP_K = 2
EXPERT_FF = 512
DEEPNORM_ALPHA = (2 * DEPTH) ** 0.25
DEEPNORM_BETA = (8 * DEPTH) ** -0.25
LN_EPS = 1e-5
RMS_EPS = 1e-6
NEG_INF = -1e30

kernel_name = 'hybrid_swa_gdn_hier_moe_deepnorm_adaln'


def layer_norm(x, g=None, b=None):
    xf = x.astype(jnp.float32)
    mu = jnp.mean(xf, axis=-1, keepdims=True)
    var = jnp.mean(jnp.square(xf - mu), axis=-1, keepdims=True)
    y = (xf - mu) * lax.rsqrt(var + LN_EPS)
    if g is not None:
        y = y * g.astype(jnp.float32) + b.astype(jnp.float32)
    return y.astype(x.dtype)


def l2_normalize(x):
    return x * lax.rsqrt(jnp.sum(jnp.square(x), axis=-1, keepdims=True) + RMS_EPS)


def causal_depthwise_conv(x, w):
    s = x.shape[1]
    xp = jnp.pad(x, ((0, 0), (CONV_WIDTH - 1, 0), (0, 0)))
    y = xp[:, 0:s] * w[0]
    for i in range(1, CONV_WIDTH):
        y = y + xp[:, i:i + s] * w[i]
    return y


def swa_sink_attention(q, k, v, sinks):
    bsz, s, _ = q.shape
    n_c = s // CHUNK
    qf = q.astype(jnp.float32).reshape(bsz, n_c, CHUNK, SWA_KV_HEADS, SWA_GROUP, SWA_HEAD_DIM)
    pad = WINDOW_CHUNKS * CHUNK

    def band(t):
        t = t.astype(jnp.float32).reshape(bsz, s, SWA_KV_HEADS, SWA_HEAD_DIM)
        tp = jnp.pad(t, ((0, 0), (pad, 0), (0, 0), (0, 0)))
        tp = tp.reshape(bsz, n_c + WINDOW_CHUNKS, CHUNK, SWA_KV_HEADS, SWA_HEAD_DIM)
        return jnp.concatenate([tp[:, j:j + n_c] for j in range(WINDOW_CHUNKS + 1)], axis=2)

    kb = band(k)
    vb = band(v)
    key_chunk = jnp.arange(n_c)[:, None] - WINDOW_CHUNKS + (jnp.arange(BAND) // CHUNK)[None, :]
    valid = key_chunk >= 0
    scores = jnp.einsum('bnqhgd,bnkhd->bnhgqk', qf, kb) * (SWA_HEAD_DIM ** -0.5)
    scores = jnp.where(valid[None, :, None, None, None, :], scores, NEG_INF)
    sink = sinks.astype(jnp.float32).reshape(1, 1, SWA_KV_HEADS, SWA_GROUP, 1, 1)
    sink = jnp.broadcast_to(sink, scores.shape[:-1] + (1,))
    probs = jax.nn.softmax(jnp.concatenate([scores, sink], axis=-1), axis=-1)[..., :-1]
    out = jnp.einsum('bnhgqk,bnkhd->bnqhgd', probs, vb)
    return out.reshape(bsz, s, SWA_Q_W).astype(v.dtype)


def gated_delta_rule(q, k, v, g, beta):
    bsz, s, h, dk = q.shape
    dv = v.shape[-1]
    n_c = s // CHUNK

    def to_chunks(t):
        return t.reshape(bsz, n_c, CHUNK, h, -1).transpose(0, 3, 1, 2, 4)

    q, k, v = to_chunks(q), to_chunks(k), to_chunks(v)
    g = g.reshape(bsz, n_c, CHUNK, h).transpose(0, 3, 1, 2)
    beta = beta.reshape(bsz, n_c, CHUNK, h).transpose(0, 3, 1, 2)
    cum = jnp.cumsum(g, axis=-1)
    idx = jnp.arange(CHUNK)
    strict = idx[:, None] > idx[None, :]
    incl = idx[:, None] >= idx[None, :]
    diff = cum[..., :, None] - cum[..., None, :]
    decay_incl = jnp.where(incl, jnp.exp(jnp.where(incl, diff, 0.0)), 0.0)
    kk = jnp.einsum('bhncd,bhnjd->bhncj', k, k)
    a_mat = jnp.where(strict, beta[..., :, None] * kk * decay_incl, 0.0)
    eye = jnp.eye(CHUNK, dtype=jnp.float32)
    t_mat = lax.linalg.triangular_solve(a_mat + eye, jnp.broadcast_to(eye, a_mat.shape),
                                        left_side=True, lower=True, unit_diagonal=True)
    u = t_mat @ (v * beta[..., None])
    w = t_mat @ (k * (beta * jnp.exp(cum))[..., None])
    qk = jnp.einsum('bhncd,bhnjd->bhncj', q, k) * decay_incl
    q_dec = q * jnp.exp(cum)[..., None]
    k_dec = k * jnp.exp(cum[..., -1:] - cum)[..., None]
    chunk_decay = jnp.exp(cum[..., -1])

    def step(state, xs):
        u_c, w_c, qk_c, qd_c, kd_c, dec_c = xs
        v_new = u_c - w_c @ state
        o_c = qd_c @ state + qk_c @ v_new
        state = dec_c[..., None, None] * state + jnp.swapaxes(kd_c, -1, -2) @ v_new
        return state, o_c

    xs = (jnp.moveaxis(u, 2, 0), jnp.moveaxis(w, 2, 0), jnp.moveaxis(qk, 2, 0),
          jnp.moveaxis(q_dec, 2, 0), jnp.moveaxis(k_dec, 2, 0), jnp.moveaxis(chunk_decay, 2, 0))
    state0 = jnp.zeros((bsz, h, dk, dv), jnp.float32)
    _, o = lax.scan(step, state0, xs)
    return o.transpose(1, 0, 3, 2, 4).reshape(bsz, s, h, dv)


def token_mixer(u, w_in, conv_w, swa_sinks, gdn_a_log, gdn_dt_bias, gdn_norm_w,
                w_proj_a, w_proj_b, w_out):
    bsz, s, _ = u.shape
    hcat = u @ w_in
    split_points = tuple(np.cumsum(IN_SIZES)[:-1].tolist())
    qa, ka, va, qb, kb, vb, zb, b_logit, a_logit, ga_logit, gb_logit = jnp.split(hcat, split_points, axis=-1)
    oa = swa_sink_attention(qa, ka, va, swa_sinks)
    qkv = jax.nn.silu(causal_depthwise_conv(jnp.concatenate([qb, kb, vb], axis=-1), conv_w))
    qb, kb, vb = jnp.split(qkv, 3, axis=-1)
    shp = (bsz, s, GDN_HEADS, GDN_HEAD_DIM)
    qh = l2_normalize(qb.astype(jnp.float32).reshape(shp)) * (GDN_HEAD_DIM ** -0.5)
    kh = l2_normalize(kb.astype(jnp.float32).reshape(shp))
    vh = vb.astype(jnp.float32).reshape(shp)
    beta = jax.nn.sigmoid(b_logit.astype(jnp.float32))
    g = -jnp.exp(gdn_a_log.astype(jnp.float32)) * jax.nn.softplus(
        a_logit.astype(jnp.float32) + gdn_dt_bias.astype(jnp.float32))
    ob = gated_delta_rule(qh, kh, vh, g, beta)
    ob = ob * lax.rsqrt(jnp.mean(jnp.square(ob), axis=-1, keepdims=True) + RMS_EPS)
    ob = ob * gdn_norm_w.astype(jnp.float32) * jax.nn.silu(zb.astype(jnp.float32).reshape(shp))
    ob = ob.reshape(bsz, s, GDN_W).astype(u.dtype)
    merged = jax.nn.sigmoid(ga_logit) * (oa @ w_proj_a) + jax.nn.sigmoid(gb_logit) * (ob @ w_proj_b)
    return merged @ w_out


def hier_moe(u, w_router_group, b_router_group, w_router_expert, b_router_expert, w_gate_up, w_down):
    bsz, s, d = u.shape
    t = u.reshape(-1, d)
    n = t.shape[0]
    p_group = jax.nn.softmax((t @ w_router_group + b_router_group).astype(jnp.float32), axis=-1)
    g_idx = jnp.argmax(p_group, axis=-1)
    p_group_sel = jnp.take_along_axis(p_group, g_idx[:, None], axis=-1)
    e_logits = (t @ w_router_expert + b_router_expert).astype(jnp.float32)
    e_logits = e_logits.reshape(n, N_GROUPS, EXPERTS_PER_GROUP)
    e_logits = jnp.take_along_axis(e_logits, g_idx[:, None, None], axis=1)[:, 0]
    p_exp = jax.nn.softmax(e_logits, axis=-1)
    top_p, top_i = lax.top_k(p_exp, TOP_K)
    top_p = top_p / jnp.sum(top_p, axis=-1, keepdims=True)
    weights = p_group_sel * top_p
    expert_idx = g_idx[:, None] * EXPERTS_PER_GROUP + top_i
    combine = jnp.sum(jax.nn.one_hot(expert_idx, N_EXPERTS, dtype=jnp.float32) * weights[..., None], axis=1)
    y = jnp.zeros((n, d), jnp.float32)
    for e in range(N_EXPERTS):
        gate, up = jnp.split(t @ w_gate_up[e], 2, axis=-1)
        y = y + combine[:, e:e + 1] * ((jax.nn.silu(gate) * up) @ w_down[e]).astype(jnp.float32)
    return y.reshape(bsz, s, d).astype(u.dtype)


def setup_inputs(seed: int = 0) -> dict:
    key = jax.random.key(seed)
    ks = jax.random.split(key, 24)
    f32 = jnp.float32
    nrm = lambda k, shape, scale: jax.random.normal(k, shape, f32) * scale
    L = DEPTH
    dt = jnp.exp(jax.random.uniform(ks[7], (L, GDN_HEADS), f32, np.log(1e-3), np.log(1e-1)))
    return {
        'x': nrm(ks[0], (BATCH, SEQ, D_MODEL), 1.0),
        'c': nrm(ks[1], (BATCH, D_MODEL), 1.0),
        'w_ada': nrm(ks[2], (L, D_MODEL, 6 * D_MODEL), 0.5 * D_MODEL ** -0.5),
        'b_ada': nrm(ks[3], (L, 6 * D_MODEL), 0.02),
        'w_in': nrm(ks[4], (L, D_MODEL, IN_WIDTH), D_MODEL ** -0.5),
        'conv_w': nrm(ks[5], (L, CONV_WIDTH, 3 * GDN_W), CONV_WIDTH ** -0.5),
        'swa_sinks': nrm(ks[6], (L, SWA_Q_HEADS), 0.5),
        'gdn_a_log': jnp.log(jax.random.uniform(ks[8], (L, GDN_HEADS), f32, 1.0, 16.0)),
        'gdn_dt_bias': dt + jnp.log(-jnp.expm1(-dt)),
        'gdn_norm_w': 1.0 + nrm(ks[9], (L, GDN_HEAD_DIM), 0.1),
        'w_proj_a': nrm(ks[10], (L, SWA_Q_W, D_MODEL), DEEPNORM_BETA * SWA_Q_W ** -0.5),
        'w_proj_b': nrm(ks[11], (L, GDN_W, D_MODEL), DEEPNORM_BETA * GDN_W ** -0.5),
        'w_out': nrm(ks[12], (L, D_MODEL, D_MODEL), DEEPNORM_BETA * D_MODEL ** -0.5),
        'ln1_g': 1.0 + nrm(ks[13], (L, D_MODEL), 0.1),
        'ln1_b': nrm(ks[14], (L, D_MODEL), 0.02),
        'w_router_group': nrm(ks[15], (L, D_MODEL, N_GROUPS), D_MODEL ** -0.5),
        'b_router_group': nrm(ks[16], (L, N_GROUPS), 0.01),
        'w_router_expert': nrm(ks[17], (L, D_MODEL, N_EXPERTS), D_MODEL ** -0.5),
        'b_router_expert': nrm(ks[18], (L, N_EXPERTS), 0.01),
        'w_gate_up': nrm(ks[19], (L, N_EXPERTS, D_MODEL, 2 * EXPERT_FF), D_MODEL ** -0.5),
        'w_down': nrm(ks[20], (L, N_EXPERTS, EXPERT_FF, D_MODEL), DEEPNORM_BETA * EXPERT_FF ** -0.5),
        'ln2_g': 1.0 + nrm(ks[21], (L, D_MODEL), 0.1),
        'ln2_b': nrm(ks[22], (L, D_MODEL), 0.02),
    }


def reference(x, c, w_ada, b_ada, w_in, conv_w, swa_sinks, gdn_a_log, gdn_dt_bias, gdn_norm_w,
              w_proj_a, w_proj_b, w_out, ln1_g, ln1_b, w_router_group, b_router_group,
              w_router_expert, b_router_expert, w_gate_up, w_down, ln2_g, ln2_b):
    for l in range(DEPTH):
        mod = (jax.nn.silu(c) @ w_ada[l] + b_ada[l])[:, None, :]
        sh1, sc1, g1, sh2, sc2, g2 = jnp.split(mod, 6, axis=-1)
        u = layer_norm(x) * (1.0 + sc1) + sh1
        mix = token_mixer(u, w_in[l], conv_w[l], swa_sinks[l], gdn_a_log[l], gdn_dt_bias[l],
                          gdn_norm_w[l], w_proj_a[l], w_proj_b[l], w_out[l])
        x = layer_norm(DEEPNORM_ALPHA * x + g1 * mix, ln1_g[l], ln1_b[l])
        u = layer_norm(x) * (1.0 + sc2) + sh2
        ffn = hier_moe(u, w_router_group[l], b_router_group[l], w_router_expert[l],
                       b_router_expert[l], w_gate_up[l], w_down[l])
        x = layer_norm(DEEPNORM_ALPHA * x + g2 * ffn, ln2_g[l], ln2_b[l])
    return x
```

```python
import functools

import jax
import jax.numpy as jnp
from jax import lax
from jax.experimental import pallas as pl
from jax.experimental.pallas import tpu as pltpu

F32 = jnp.float32
BF16 = jnp.bfloat16
I32 = jnp.int32

D = 2048
CHUNK = 64
Q_HEADS = 16
KV_HEADS = 4
HEAD_DIM = 64
SWA_Q_W = 1024
SWA_KV_W = 256
GDN_HEADS = 8
GDN_DIM = 128
GDN_W = 1024
CONV_WIDTH = 4
N_GROUPS = 4
EPG = 8
N_EXPERTS = 32
EXPERT_FF = 512
DEPTH = 1
ALPHA = (2 * DEPTH) ** 0.25
LN_EPS = 1e-5
RMS_EPS = 1e-6
NEG_INF = -1e30

LANES = 128
VMEM_LIMIT = 56 * 1024 * 1024

COL_QA = 0
COL_K2 = 1024
COL_V2 = 1536
COL_QB = 2048
COL_KB = 3072
COL_VB = 4096
COL_ZB = 5120
COL_GA = 6144
COL_GB = 8192
H_WIDTH = 10240

ROUTE_EXPERT_LANE = 32
MOE_TILE = 256


def _dot(a, b):
    return jnp.dot(a, b, preferred_element_type=F32)


def _dot_nt(a, b):
    return lax.dot_general(a, b, (((1,), (1,)), ((), ())), preferred_element_type=F32)


def _dot_tn(a, b):
    return lax.dot_general(a, b, (((0,), (0,)), ((), ())), preferred_element_type=F32)


def _split_bf16(v):
    hi = v.astype(BF16)
    lo = (v - hi.astype(F32)).astype(BF16)
    return hi, lo


def _silu(v):
    return v * jax.nn.sigmoid(v)


def _softplus(v):
    return jnp.maximum(v, 0.0) + jnp.log(1.0 + jnp.exp(-jnp.abs(v)))


def _layer_norm(v):
    mu = jnp.mean(v, axis=-1, keepdims=True)
    vc = v - mu
    var = jnp.mean(vc * vc, axis=-1, keepdims=True)
    return vc * lax.rsqrt(var + LN_EPS)


def _params(*sem):
    return pltpu.CompilerParams(dimension_semantics=sem, vmem_limit_bytes=VMEM_LIMIT)


def _ada_kernel(c_ref, w_ref, b_ref, o_ref):
    s_hi, s_lo = _split_bf16(_silu(c_ref[...]))
    w_hi, w_lo = _split_bf16(w_ref[...])
    o_ref[...] = _dot(s_hi, w_hi) + _dot(s_lo, w_hi) + _dot(s_hi, w_lo) + b_ref[...]


def _ada(c, w_ada, b_ada):
    bsz = c.shape[0]
    width = w_ada.shape[1]
    tn = 1024
    return pl.pallas_call(
        _ada_kernel,
        out_shape=jax.ShapeDtypeStruct((bsz, width), F32),
        grid=(width // tn,),
        in_specs=[pl.BlockSpec((bsz, D), lambda j: (0, 0)),
                  pl.BlockSpec((D, tn), lambda j: (0, j)),
                  pl.BlockSpec((1, tn), lambda j: (0, j))],
        out_specs=pl.BlockSpec((bsz, tn), lambda j: (0, j)),
        compiler_params=_params("arbitrary"),
        name="ada",
    )(c, w_ada, b_ada.reshape(1, width))


def _inproj_kernel(x_ref, mod_ref, w_ref, ws_ref, wst_ref, h_ref, sm_ref, smt_ref, u_scr):
    @pl.when(pl.program_id(1) == 0)
    def _():
        mod = mod_ref[0]
        u = _layer_norm(x_ref[...]) * (1.0 + mod[1:2]) + mod[0:1]
        ub = u.astype(BF16)
        u_scr[...] = ub
        sm_ref[...] = _dot(ub, ws_ref[...])
        smt_ref[...] = _dot_nt(wst_ref[...], ub)

    h_ref[...] = _dot(u_scr[...], w_ref[...]).astype(BF16)


def _inproj(x2, mod3, w_main, w_small, w_small_t, seq):
    n = x2.shape[0]
    tm = min(1024, seq)
    tn = 1024
    tiles_per_batch = seq // tm
    return pl.pallas_call(
        _inproj_kernel,
        out_shape=(jax.ShapeDtypeStruct((n, H_WIDTH), BF16),
                   jax.ShapeDtypeStruct((n, LANES), F32),
                   jax.ShapeDtypeStruct((2 * GDN_HEADS, n), F32)),
        grid=(n // tm, H_WIDTH // tn),
        in_specs=[pl.BlockSpec((tm, D), lambda i, j: (i, 0)),
                  pl.BlockSpec((1, 6, D), lambda i, j: (i // tiles_per_batch, 0, 0)),
                  pl.BlockSpec((D, tn), lambda i, j: (0, j)),
                  pl.BlockSpec((D, LANES), lambda i, j: (0, 0)),
                  pl.BlockSpec((2 * GDN_HEADS, D), lambda i, j: (0, 0))],
        out_specs=(pl.BlockSpec((tm, tn), lambda i, j: (i, j)),
                   pl.BlockSpec((tm, LANES), lambda i, j: (i, 0)),
                   pl.BlockSpec((2 * GDN_HEADS, tm), lambda i, j: (0, i))),
        scratch_shapes=[pltpu.VMEM((tm, D), BF16)],
        compiler_params=_params("arbitrary", "arbitrary"),
        name="inproj",
    )(x2, mod3, w_main, w_small, w_small_t)


SWA_TQ = 256
SWA_PREV = 128
SWA_BAND = 192


def _swa_kernel(sink_ref, q_ref, kp_ref, vp_ref, kc_ref, vc_ref, o_ref):
    i = pl.program_id(1)
    kwin = jnp.concatenate([kp_ref[...], kc_ref[...]], axis=0)
    vwin = jnp.concatenate([vp_ref[...], vc_ref[...]], axis=0)
    lo_lane = lax.broadcasted_iota(I32, (1, LANES), 1) < HEAD_DIM
    row_top = lax.broadcasted_iota(I32, (2 * CHUNK, 1), 0) < CHUNK
    key_iota = lax.broadcasted_iota(I32, (1, SWA_BAND), 1)
    zero = jnp.zeros((), BF16)
    for kv in range(KV_HEADS):
        k2 = kwin[:, kv * LANES:(kv + 1) * LANES]
        v2 = vwin[:, kv * LANES:(kv + 1) * LANES]
        k_lo = jnp.where(lo_lane, k2, zero)
        k_hi = jnp.where(lo_lane, zero, k2)
        v_lo = jnp.where(lo_lane, v2, zero)
        v_hi = jnp.where(lo_lane, zero, v2)
        sink_e = jnp.where(row_top, sink_ref[kv * 4 + 0], sink_ref[kv * 4 + 2])
        sink_o = jnp.where(row_top, sink_ref[kv * 4 + 1], sink_ref[kv * 4 + 3])
        for c in range(SWA_TQ // CHUNK):
            rows = slice(c * CHUNK, (c + 1) * CHUNK)
            band = slice(c * CHUNK, c * CHUNK + SWA_BAND)
            base = kv * 2 * LANES
            ql = jnp.concatenate([q_ref[rows, base:base + LANES],
                                  q_ref[rows, base + LANES:base + 2 * LANES]], axis=0)
            valid = (i * SWA_TQ - SWA_PREV + c * CHUNK + key_iota) >= 0

            def probs(k_half, sink):
                s = _dot_nt(ql, k_half[band]) * (HEAD_DIM ** -0.5)
                s = jnp.where(valid, s, NEG_INF)
                m = jnp.maximum(jnp.max(s, axis=-1, keepdims=True), sink)
                p = jnp.exp(s - m)
                den = jnp.sum(p, axis=-1, keepdims=True) + jnp.exp(sink - m)
                return p.astype(BF16), den

            p_e, den_e = probs(k_lo, sink_e)
            p_o, den_o = probs(k_hi, sink_o)
            acc = _dot(p_e, v_lo[band]) / den_e + _dot(p_o, v_hi[band]) / den_o
            o_ref[rows, base:base + LANES] = acc[0:CHUNK].astype(BF16)
            o_ref[rows, base + LANES:base + 2 * LANES] = acc[CHUNK:2 * CHUNK].astype(BF16)


def _swa(hcat, sinks, bsz, seq):
    n = hcat.shape[0]
    tq = SWA_TQ
    nq = seq // tq
    kvw = 2 * SWA_KV_W

    def prev_map(b, i, s):
        return (b * (seq // SWA_PREV) + jnp.maximum(i * (tq // SWA_PREV) - 1, 0), COL_K2 // kvw)

    def prev_map_v(b, i, s):
        return (b * (seq // SWA_PREV) + jnp.maximum(i * (tq // SWA_PREV) - 1, 0), COL_V2 // kvw)

    return pl.pallas_call(
        _swa_kernel,
        out_shape=jax.ShapeDtypeStruct((n, SWA_Q_W), BF16),
        grid_spec=pltpu.PrefetchScalarGridSpec(
            num_scalar_prefetch=1,
            grid=(bsz, nq),
            in_specs=[pl.BlockSpec((tq, SWA_Q_W), lambda b, i, s: (b * nq + i, COL_QA // SWA_Q_W)),
                      pl.BlockSpec((SWA_PREV, kvw), prev_map),
                      pl.BlockSpec((SWA_PREV, kvw), prev_map_v),
                      pl.BlockSpec((tq, kvw), lambda b, i, s: (b * nq + i, COL_K2 // kvw)),
                      pl.BlockSpec((tq, kvw), lambda b, i, s: (b * nq + i, COL_V2 // kvw))],
            out_specs=pl.BlockSpec((tq, SWA_Q_W), lambda b, i, s: (b * nq + i, 0))),
        compiler_params=_params("arbitrary", "arbitrary"),
        name="swa",
    )(sinks, hcat, hcat, hcat, hcat, hcat)


GDN_PREV = 16


def _unit_lower_inverse(a, masks):
    eye, blk8, blk16, blk32 = masks

    def mm(p, q):
        return _dot(p.astype(BF16), q.astype(BF16))

    a8 = jnp.where(blk8, a, 0.0)
    a8_2 = mm(a8, a8)
    a8_4 = mm(a8_2, a8_2)
    t = mm(eye - a8, eye + a8_2)
    t = mm(t, eye + a8_4)
    for inner, outer in ((blk8, blk16), (blk16, blk32), (blk32, None)):
        off = jnp.where(inner, 0.0, a) if outer is None else jnp.where(outer & ~inner, a, 0.0)
        t = t - mm(t, mm(off, t))
    return t


def _gdn_kernel(qc_ref, kc_ref, vc_ref, z_ref, qp_ref, kp_ref, vp_ref, cw_ref, sm_ref, smt_ref,
                prow_ref, pcol_ref, nw_ref, o_ref, state, xbuf):
    n = pl.program_id(1)

    @pl.when(n == 0)
    def _():
        state[...] = jnp.zeros_like(state)

    keep = jnp.where(n == 0, 0.0, 1.0)
    for s, (cur, prev) in enumerate(((qc_ref, qp_ref), (kc_ref, kp_ref), (vc_ref, vp_ref))):
        cols = slice(s * GDN_W, (s + 1) * GDN_W)
        xbuf[0:GDN_PREV, cols] = prev[...].astype(F32) * keep
        xbuf[GDN_PREV:GDN_PREV + CHUNK, cols] = cur[...].astype(F32)
    conv = None
    for tap in range(CONV_WIDTH):
        start = GDN_PREV - (CONV_WIDTH - 1) + tap
        term = xbuf[start:start + CHUNK, :] * cw_ref[tap:tap + 1, :]
        conv = term if conv is None else conv + term
    qkv = _silu(conv)

    ri = lax.broadcasted_iota(I32, (CHUNK, CHUNK), 0)
    ci = lax.broadcasted_iota(I32, (CHUNK, CHUNK), 1)
    incl = ri >= ci
    strict = ri > ci
    eye = jnp.where(ri == ci, 1.0, 0.0).astype(F32)
    masks = (eye, (ri // 8) == (ci // 8), (ri // 16) == (ci // 16), (ri // 32) == (ci // 32))
    tri_lo = jnp.where(incl, 1.0, 0.0).astype(BF16)
    tri_up = jnp.where(ri <= ci, 1.0, 0.0).astype(BF16)

    sm = sm_ref[...]
    beta_all = jax.nn.sigmoid(sm)
    g_all = -jnp.exp(prow_ref[0:1, :]) * _softplus(sm + prow_ref[1:2, :])
    g_hi, g_lo = _split_bf16(g_all)
    cum_all = _dot(tri_lo, g_hi) + _dot(tri_lo, g_lo)
    ecum_all = jnp.exp(cum_all)
    cum_last = cum_all[CHUNK - 1:CHUNK, :]
    kscale_all = jnp.exp(cum_last - cum_all)
    cdec_all = jnp.exp(cum_last)
    smt = smt_ref[0]
    g_row = -jnp.exp(pcol_ref[:, 0:1]) * _softplus(smt[GDN_HEADS:2 * GDN_HEADS, :] + pcol_ref[:, 1:2])
    gr_hi, gr_lo = _split_bf16(g_row)
    cum_row = _dot(gr_hi, tri_up) + _dot(gr_lo, tri_up)

    for h in range(GDN_HEADS):
        lanes = slice(h * GDN_DIM, (h + 1) * GDN_DIM)
        qh = qkv[:, h * GDN_DIM:(h + 1) * GDN_DIM]
        kh = qkv[:, GDN_W + h * GDN_DIM:GDN_W + (h + 1) * GDN_DIM]
        vh = qkv[:, 2 * GDN_W + h * GDN_DIM:2 * GDN_W + (h + 1) * GDN_DIM]
        qh = qh * lax.rsqrt(jnp.sum(qh * qh, axis=-1, keepdims=True) + RMS_EPS) * (GDN_DIM ** -0.5)
        kh = kh * lax.rsqrt(jnp.sum(kh * kh, axis=-1, keepdims=True) + RMS_EPS)
        beta = beta_all[:, h:h + 1]
        cum_c = cum_all[:, GDN_HEADS + h:GDN_HEADS + h + 1]
        ecum = ecum_all[:, GDN_HEADS + h:GDN_HEADS + h + 1]
        kscale = kscale_all[:, GDN_HEADS + h:GDN_HEADS + h + 1]
        cdec = cdec_all[:, GDN_HEADS + h:GDN_HEADS + h + 1]
        diff = cum_c - cum_row[h:h + 1, :]
        dec = jnp.where(incl, jnp.exp(jnp.where(incl, diff, 0.0)), 0.0)

        kb = kh.astype(BF16)
        qkk = _dot_nt(jnp.concatenate([qh.astype(BF16), kb], axis=0), kb)
        qk = qkk[0:CHUNK] * dec
        a_mat = jnp.where(strict, beta * qkk[CHUNK:2 * CHUNK] * dec, 0.0)
        t_mat = _unit_lower_inverse(a_mat, masks)
        rhs = jnp.concatenate([vh * beta, kh * (beta * ecum)], axis=1)
        uw = _dot(t_mat.astype(BF16), rhs.astype(BF16))
        u_c = uw[:, 0:GDN_DIM]
        w_c = uw[:, GDN_DIM:2 * GDN_DIM]
        q_dec = qh * ecum
        k_dec = kh * kscale

        s_old = state[h]
        ws = _dot(jnp.concatenate([w_c, q_dec], axis=0).astype(BF16), s_old.astype(BF16))
        v_new = u_c - ws[0:CHUNK]
        vb = v_new.astype(BF16)
        o_c = ws[CHUNK:2 * CHUNK] + _dot(qk.astype(BF16), vb)
        state[h] = cdec * s_old + _dot_tn(k_dec.astype(BF16), vb)

        o_n = o_c * lax.rsqrt(jnp.mean(o_c * o_c, axis=-1, keepdims=True) + RMS_EPS)
        o_n = o_n * nw_ref[...] * _silu(z_ref[:, lanes].astype(F32))
        o_ref[:, lanes] = o_n.astype(BF16)


def _gdn(hcat, sm, smt3, conv_w, prow, pcol, norm_w, bsz, seq):
    n = hcat.shape[0]
    nc = seq // CHUNK
    pb = CHUNK // GDN_PREV

    def cur(col):
        return pl.BlockSpec((CHUNK, GDN_W), lambda b, c: (b * nc + c, col // GDN_W))

    def prev(col):
        return pl.BlockSpec((GDN_PREV, GDN_W),
                            lambda b, c: (b * nc * pb + jnp.maximum(c * pb - 1, 0), col // GDN_W))

    return pl.pallas_call(
        _gdn_kernel,
        out_shape=jax.ShapeDtypeStruct((n, GDN_W), BF16),
        grid=(bsz, nc),
        in_specs=[cur(COL_QB), cur(COL_KB), cur(COL_VB), cur(COL_ZB),
                  prev(COL_QB), prev(COL_KB), prev(COL_VB),
                  pl.BlockSpec((CONV_WIDTH, 3 * GDN_W), lambda b, c: (0, 0)),
                  pl.BlockSpec((CHUNK, LANES), lambda b, c: (b * nc + c, 0)),
                  pl.BlockSpec((1, 2 * GDN_HEADS, CHUNK), lambda b, c: (b * nc + c, 0, 0)),
                  pl.BlockSpec((2, LANES), lambda b, c: (0, 0)),
                  pl.BlockSpec((GDN_HEADS, LANES), lambda b, c: (0, 0)),
                  pl.BlockSpec((1, GDN_DIM), lambda b, c: (0, 0))],
        out_specs=pl.BlockSpec((CHUNK, GDN_W), lambda b, c: (b * nc + c, 0)),
        scratch_shapes=[pltpu.VMEM((GDN_HEADS, GDN_DIM, GDN_DIM), F32),
                        pltpu.VMEM((GDN_PREV + CHUNK, 3 * GDN_W), F32)],
        compiler_params=_params("arbitrary", "arbitrary"),
        name="gdn",
    )(hcat, hcat, hcat, hcat, hcat, hcat, hcat, conv_w, sm, smt3, prow, pcol, norm_w)


def _outproj_kernel(oa_ref, ob_ref, ga_ref, gb_ref, x_ref, mod_ref, wa_ref, wb_ref, wo_ref,
                    g1_ref, b1_ref, wr_ref, br_ref,
                    x1_ref, u2_ref, ri_ref, rw_ref, cnt_ref, run):
    i = pl.program_id(0)

    @pl.when(i == 0)
    def _():
        run[...] = jnp.zeros_like(run)

    mod = mod_ref[0]
    pa = _dot(oa_ref[...], wa_ref[...])
    pb = _dot(ob_ref[...], wb_ref[...])
    merged = jax.nn.sigmoid(ga_ref[...].astype(F32)) * pa + jax.nn.sigmoid(gb_ref[...].astype(F32)) * pb
    mix = _dot(merged.astype(BF16), wo_ref[...])
    x1 = _layer_norm(ALPHA * x_ref[...] + mod[2:3] * mix) * g1_ref[...] + b1_ref[...]
    x1_ref[...] = x1
    u2 = _layer_norm(x1) * (1.0 + mod[4:5]) + mod[3:4]
    u2_ref[...] = u2

    u_hi, u_lo = _split_bf16(u2)
    w_hi, w_lo = _split_bf16(wr_ref[...])
    logits = _dot(u_hi, w_hi) + _dot(u_lo, w_hi) + _dot(u_hi, w_lo) + br_ref[...]

    tm = logits.shape[0]
    lane = lax.broadcasted_iota(I32, (1, LANES), 1)
    big = jnp.int32(LANES)
    gmask = lane < N_GROUPS
    lg = jnp.where(gmask, logits, NEG_INF)
    gmax = jnp.max(lg, axis=-1, keepdims=True)
    gidx = jnp.min(jnp.where(lg == gmax, lane, big), axis=-1, keepdims=True)
    p_group = 1.0 / jnp.sum(jnp.exp(lg - gmax), axis=-1, keepdims=True)
    emask = (lane >> 3) == (gidx + ROUTE_EXPERT_LANE // EPG)
    le = jnp.where(emask, logits, NEG_INF)
    m1 = jnp.max(le, axis=-1, keepdims=True)
    i1 = jnp.min(jnp.where(le == m1, lane, big), axis=-1, keepdims=True)
    le2 = jnp.where(lane == i1, NEG_INF, le)
    m2 = jnp.max(le2, axis=-1, keepdims=True)
    i2 = jnp.min(jnp.where(le2 == m2, lane, big), axis=-1, keepdims=True)
    e2_rel = jnp.exp(m2 - m1)
    wgt1 = p_group / (1.0 + e2_rel)
    wgt2 = p_group * e2_rel / (1.0 + e2_rel)
    e1 = i1 - ROUTE_EXPERT_LANE
    e2 = i2 - ROUTE_EXPERT_LANE

    hot1 = lane == e1
    hot2 = lane == e2
    onehot = jnp.where(hot1 | hot2, 1.0, 0.0).astype(F32)
    tr = lax.broadcasted_iota(I32, (tm, tm), 0)
    tc = lax.broadcasted_iota(I32, (tm, tm), 1)
    before = jnp.where(tr > tc, 1.0, 0.0).astype(BF16)
    total = run[...] + _dot(before, onehot.astype(BF16))
    r1 = jnp.sum(jnp.where(hot1, total, 0.0), axis=-1, keepdims=True).astype(I32)
    r2 = jnp.sum(jnp.where(hot2, total, 0.0), axis=-1, keepdims=True).astype(I32)
    run[...] = run[...] + jnp.sum(onehot, axis=0, keepdims=True)
    cnt_ref[...] = run[...]

    ri_ref[...] = jnp.where(lane == 0, e1, jnp.where(lane == 1, e2, jnp.where(lane == 2, r1, r2)))
    rw_ref[...] = jnp.where(lane == 0, wgt1, wgt2)


def _outproj(oa, ob, hcat, x2, mod3, wa, wb, wo, ln_g, ln_b, w_route, b_route, seq):
    n = x2.shape[0]
    tm = 256
    tiles_per_batch = seq // tm
    const = dict(pipeline_mode=pl.Buffered(1))
    return pl.pallas_call(
        _outproj_kernel,
        out_shape=(jax.ShapeDtypeStruct((n, D), F32),
                   jax.ShapeDtypeStruct((n, D), F32),
                   jax.ShapeDtypeStruct((n, LANES), I32),
                   jax.ShapeDtypeStruct((n, LANES), F32),
                   jax.ShapeDtypeStruct((1, LANES), F32)),
        grid=(n // tm,),
        in_specs=[pl.BlockSpec((tm, SWA_Q_W), lambda i: (i, 0)),
                  pl.BlockSpec((tm, GDN_W), lambda i: (i, 0)),
                  pl.BlockSpec((tm, D), lambda i: (i, COL_GA // D)),
                  pl.BlockSpec((tm, D), lambda i: (i, COL_GB // D)),
                  pl.BlockSpec((tm, D), lambda i: (i, 0)),
                  pl.BlockSpec((1, 6, D), lambda i: (i // tiles_per_batch, 0, 0)),
                  pl.BlockSpec((SWA_Q_W, D), lambda i: (0, 0), **const),
                  pl.BlockSpec((GDN_W, D), lambda i: (0, 0), **const),
                  pl.BlockSpec((D, D), lambda i: (0, 0), **const),
                  pl.BlockSpec((1, D), lambda i: (0, 0)),
                  pl.BlockSpec((1, D), lambda i: (0, 0)),
                  pl.BlockSpec((D, LANES), lambda i: (0, 0), **const),
                  pl.BlockSpec((1, LANES), lambda i: (0, 0))],
        out_specs=(pl.BlockSpec((tm, D), lambda i: (i, 0)),
                   pl.BlockSpec((tm, D), lambda i: (i, 0)),
                   pl.BlockSpec((tm, LANES), lambda i: (i, 0)),
                   pl.BlockSpec((tm, LANES), lambda i: (i, 0)),
                   pl.BlockSpec((1, LANES), lambda i: (0, 0))),
        scratch_shapes=[pltpu.VMEM((1, LANES), F32)],
        compiler_params=_params("arbitrary"),
        name="outproj",
    )(oa, ob, hcat, hcat, x2, mod3, wa, wb, wo, ln_g, ln_b, w_route, b_route)


DISPATCH_TM = 256


def _dispatch_kernel(pos_ref, fill_ref, u_ref, xs_ref, zeros, sem, fill_sem):
    base = pl.program_id(0) * (2 * DISPATCH_TM)
    n_tiles = xs_ref.shape[0] // MOE_TILE

    @pl.when(pl.program_id(0) == 0)
    def _():
        zeros[...] = jnp.zeros_like(zeros)

        def fill_copy(t):
            return pltpu.make_async_copy(zeros, xs_ref.at[pl.ds(t * MOE_TILE, MOE_TILE), :], fill_sem)

        def fill_start(t, carry):
            @pl.when(fill_ref[t] == 1)
            def _():
                fill_copy(t).start()
            return carry

        def fill_wait(t, carry):
            @pl.when(fill_ref[t] == 1)
            def _():
                fill_copy(t).wait()
            return carry

        lax.fori_loop(0, n_tiles, fill_start, 0)
        lax.fori_loop(0, n_tiles, fill_wait, 0)

    def row_copy(r, k):
        dst = pos_ref[base + 2 * r + k]
        return pltpu.make_async_copy(u_ref.at[pl.ds(r, 1), :], xs_ref.at[pl.ds(dst, 1), :], sem)

    def issue(r, carry):
        row_copy(r, 0).start()
        row_copy(r, 1).start()
        return carry

    def drain(r, carry):
        row_copy(r, 0).wait()
        row_copy(r, 1).wait()
        return carry

    lax.fori_loop(0, DISPATCH_TM, issue, 0)
    lax.fori_loop(0, DISPATCH_TM, drain, 0)


def _dispatch(pos, tile_fill, u2, n_pad):
    n = u2.shape[0]
    return pl.pallas_call(
        _dispatch_kernel,
        out_shape=jax.ShapeDtypeStruct((n_pad, D), F32),
        grid_spec=pltpu.PrefetchScalarGridSpec(
            num_scalar_prefetch=2,
            grid=(n // DISPATCH_TM,),
            in_specs=[pl.BlockSpec((DISPATCH_TM, D), lambda i, p, f: (i, 0))],
            out_specs=pl.BlockSpec(memory_space=pl.ANY),
            scratch_shapes=[pltpu.VMEM((MOE_TILE, D), F32),
                            pltpu.SemaphoreType.DMA(()),
                            pltpu.SemaphoreType.DMA(())]),
        compiler_params=_params("arbitrary"),
        name="dispatch",
    )(pos, tile_fill, u2)


def _moe_kernel(te_ref, tr_ref, tv_ref, x_ref, wgu_ref, wd_ref, y_ref, wgu_bf, wd_bf):
    i = pl.program_id(0)
    fresh = jnp.logical_or(i == 0, te_ref[i] != te_ref[jnp.maximum(i - 1, 0)])

    @pl.when(jnp.logical_and(fresh, tv_ref[i] == 1))
    def _():
        wgu_bf[...] = wgu_ref[0].astype(BF16)
        wd_bf[...] = wd_ref[0].astype(BF16)

    @pl.when(tv_ref[i] == 1)
    def _():
        gu = _dot(x_ref[...].astype(BF16), wgu_bf[...])
        hid = _silu(gu[:, 0:EXPERT_FF]) * gu[:, EXPERT_FF:2 * EXPERT_FF]
        y_ref[...] = _dot(hid.astype(BF16), wd_bf[...])

    @pl.when(tv_ref[i] == 0)
    def _():
        y_ref[...] = jnp.zeros_like(y_ref)


def _moe(tile_expert, tile_row, tile_valid, xs, w_gate_up, w_down):
    n_pad = xs.shape[0]
    n_tiles = n_pad // MOE_TILE
    return pl.pallas_call(
        _moe_kernel,
        out_shape=jax.ShapeDtypeStruct((n_pad, D), F32),
        grid_spec=pltpu.PrefetchScalarGridSpec(
            num_scalar_prefetch=3,
            grid=(n_tiles,),
            in_specs=[pl.BlockSpec((MOE_TILE, D), lambda i, te, tr, tv: (tr[i], 0)),
                      pl.BlockSpec((1, D, 2 * EXPERT_FF), lambda i, te, tr, tv: (te[i], 0, 0)),
                      pl.BlockSpec((1, EXPERT_FF, D), lambda i, te, tr, tv: (te[i], 0, 0))],
            out_specs=pl.BlockSpec((MOE_TILE, D), lambda i, te, tr, tv: (i, 0)),
            scratch_shapes=[pltpu.VMEM((D, 2 * EXPERT_FF), BF16),
                            pltpu.VMEM((EXPERT_FF, D), BF16)]),
        compiler_params=_params("arbitrary"),
        name="moe",
    )(tile_expert, tile_row, tile_valid, xs, w_gate_up, w_down)


COMBINE_TM = 256


def _combine_kernel(pos_ref, ys_ref, rw_ref, x1_ref, mod_ref, g2_ref, b2_ref, o_ref, buf, sem):
    base = pl.program_id(0) * (2 * COMBINE_TM)

    def row_copy(r, k):
        src = pos_ref[base + 2 * r + k]
        return pltpu.make_async_copy(ys_ref.at[pl.ds(src, 1), :], buf.at[k, pl.ds(r, 1), :], sem)

    def issue(r, carry):
        row_copy(r, 0).start()
        row_copy(r, 1).start()
        return carry

    def drain(r, carry):
        row_copy(r, 0).wait()
        row_copy(r, 1).wait()
        return carry

    lax.fori_loop(0, COMBINE_TM, issue, 0)
    lax.fori_loop(0, COMBINE_TM, drain, 0)

    rw = rw_ref[...]
    ffn = rw[:, 0:1] * buf[0] + rw[:, 1:2] * buf[1]
    mod = mod_ref[0]
    o_ref[...] = _layer_norm(ALPHA * x1_ref[...] + mod[5:6] * ffn) * g2_ref[...] + b2_ref[...]


def _combine(pos, ys, rw, x1, mod3, ln_g, ln_b, seq):
    n = x1.shape[0]
    tm = COMBINE_TM
    tiles_per_batch = seq // tm
    return pl.pallas_call(
        _combine_kernel,
        out_shape=jax.ShapeDtypeStruct((n, D), F32),
        grid_spec=pltpu.PrefetchScalarGridSpec(
            num_scalar_prefetch=1,
            grid=(n // tm,),
            in_specs=[pl.BlockSpec(memory_space=pl.ANY),
                      pl.BlockSpec((tm, LANES), lambda i, p: (i, 0)),
                      pl.BlockSpec((tm, D), lambda i, p: (i, 0)),
                      pl.BlockSpec((1, 6, D), lambda i, p: (i // tiles_per_batch, 0, 0)),
                      pl.BlockSpec((1, D), lambda i, p: (0, 0)),
                      pl.BlockSpec((1, D), lambda i, p: (0, 0))],
            out_specs=pl.BlockSpec((tm, D), lambda i, p: (i, 0)),
            scratch_shapes=[pltpu.VMEM((2, tm, D), F32),
                            pltpu.SemaphoreType.DMA(())]),
        compiler_params=_params("arbitrary"),
        name="combine",
    )(pos, ys, rw, x1, mod3, ln_g, ln_b)


def _in_weights(w_in):
    sizes = (SWA_Q_W, SWA_KV_W, SWA_KV_W, GDN_W, GDN_W, GDN_W, GDN_W, GDN_HEADS, GDN_HEADS, D, D)
    offs = [0]
    for s in sizes:
        offs.append(offs[-1] + s)
    qa, ka, va, qb, kb, vb, zb, bl, al, ga, gb = (w_in[:, offs[k]:offs[k + 1]] for k in range(len(sizes)))

    def dup(w):
        w4 = w.reshape(D, KV_HEADS, 1, HEAD_DIM)
        return jnp.broadcast_to(w4, (D, KV_HEADS, 2, HEAD_DIM)).reshape(D, 2 * SWA_KV_W)

    w_main = jnp.concatenate([qa, dup(ka), dup(va), qb, kb, vb, zb, ga, gb], axis=1).astype(BF16)
    small = jnp.concatenate([bl, al], axis=1)
    w_small = jnp.pad(small, ((0, 0), (0, LANES - 2 * GDN_HEADS))).astype(BF16)
    w_small_t = small.T.astype(BF16)
    return w_main, w_small, w_small_t


def _route_plan(ri, cnt, n_tiles):
    counts = cnt[0, :N_EXPERTS].astype(I32)
    tiles = (counts + MOE_TILE - 1) // MOE_TILE
    tile_end = jnp.cumsum(tiles)
    row_off = (tile_end - tiles) * MOE_TILE
    pos = (row_off[ri[:, 0:2]] + ri[:, 2:4]).reshape(-1)
    used = tile_end[-1]
    t = jnp.arange(n_tiles, dtype=I32)
    t_eff = jnp.minimum(t, used - 1)
    tile_expert = jnp.minimum(jnp.searchsorted(tile_end, t_eff, side="right"), N_EXPERTS - 1).astype(I32)
    tile_valid = (t < used).astype(I32)
    last_of_expert = jnp.any((t[:, None] + 1 == tile_end[None, :]) & (tiles[None, :] > 0), axis=1)
    tile_fill = jnp.logical_or(last_of_expert, t >= used).astype(I32)
    return pos.astype(I32), tile_expert, t_eff.astype(I32), tile_valid, tile_fill


def kernel(x, c, w_ada, b_ada, w_in, conv_w, swa_sinks, gdn_a_log, gdn_dt_bias, gdn_norm_w, w_proj_a, w_proj_b, w_out, ln1_g, ln1_b, w_router_group, b_router_group, w_router_expert, b_router_expert, w_gate_up, w_down, ln2_g, ln2_b):
    bsz, seq, _ = x.shape
    n = bsz * seq
    nc = seq // CHUNK
    for l in range(DEPTH):
        x2 = x.reshape(n, D)
        mod3 = _ada(c, w_ada[l], b_ada[l]).reshape(bsz, 6, D)
        w_main, w_small, w_small_t = _in_weights(w_in[l])
        hcat, sm, smt = _inproj(x2, mod3, w_main, w_small, w_small_t, seq)
        smt3 = smt.reshape(2 * GDN_HEADS, bsz * nc, CHUNK).transpose(1, 0, 2)

        oa = _swa(hcat, swa_sinks[l], bsz, seq)

        pad_lo = jnp.zeros((GDN_HEADS,), F32)
        prow = jnp.stack([jnp.pad(jnp.concatenate([pad_lo, gdn_a_log[l]]), (0, LANES - 2 * GDN_HEADS)),
                          jnp.pad(jnp.concatenate([pad_lo, gdn_dt_bias[l]]), (0, LANES - 2 * GDN_HEADS))])
        pcol = jnp.pad(jnp.stack([gdn_a_log[l], gdn_dt_bias[l]], axis=1), ((0, 0), (0, LANES - 2)))
        ob = _gdn(hcat, sm, smt3, conv_w[l], prow, pcol, gdn_norm_w[l].reshape(1, GDN_DIM), bsz, seq)

        w_route = jnp.zeros((D, LANES), F32)
        w_route = w_route.at[:, 0:N_GROUPS].set(w_router_group[l])
        w_route = w_route.at[:, ROUTE_EXPERT_LANE:ROUTE_EXPERT_LANE + N_EXPERTS].set(w_router_expert[l])
        b_route = jnp.zeros((1, LANES), F32)
        b_route = b_route.at[0, 0:N_GROUPS].set(b_router_group[l])
        b_route = b_route.at[0, ROUTE_EXPERT_LANE:ROUTE_EXPERT_LANE + N_EXPERTS].set(b_router_expert[l])
        x1, u2, ri, rw, cnt = _outproj(
            oa, ob, hcat, x2, mod3, w_proj_a[l].astype(BF16), w_proj_b[l].astype(BF16),
            w_out[l].astype(BF16), ln1_g[l].reshape(1, D), ln1_b[l].reshape(1, D), w_route, b_route, seq)

        n_tiles = (2 * n) // MOE_TILE + N_EXPERTS
        pos, tile_expert, tile_row, tile_valid, tile_fill = _route_plan(ri, cnt, n_tiles)
        xs = _dispatch(pos, tile_fill, u2, n_tiles * MOE_TILE)
        ys = _moe(tile_expert, tile_row, tile_valid, xs, w_gate_up[l], w_down[l])
        x2 = _combine(pos, ys, rw, x1, mod3, ln2_g[l].reshape(1, D), ln2_b[l].reshape(1, D), seq)
        x = x2.reshape(bsz, seq, D)
    return x
```

```python
import functools

import jax
import jax.numpy as jnp
from jax import lax
from jax.experimental import pallas as pl
from jax.experimental.pallas import tpu as pltpu

F32 = jnp.float32
BF16 = jnp.bfloat16
I32 = jnp.int32

D = 2048
CHUNK = 64
Q_HEADS = 16
KV_HEADS = 4
HEAD_DIM = 64
SWA_Q_W = 1024
SWA_KV_W = 256
GDN_HEADS = 8
GDN_DIM = 128
GDN_W = 1024
CONV_WIDTH = 4
N_GROUPS = 4
EPG = 8
N_EXPERTS = 32
EXPERT_FF = 512
DEPTH = 1
ALPHA = (2 * DEPTH) ** 0.25
LN_EPS = 1e-5
RMS_EPS = 1e-6
NEG_INF = -1e30

LANES = 128
VMEM_LIMIT = 56 * 1024 * 1024

COL_QA = 0
COL_K2 = 1024
COL_V2 = 1536
COL_QB = 2048
COL_KB = 3072
COL_VB = 4096
COL_ZB = 5120
COL_GA = 6144
COL_GB = 8192
H_WIDTH = 10240

ROUTE_EXPERT_LANE = 32
MOE_TILE = 256


def _dot(a, b):
    return jnp.dot(a, b, preferred_element_type=F32)


def _dot_nt(a, b):
    return lax.dot_general(a, b, (((1,), (1,)), ((), ())), preferred_element_type=F32)


def _dot_tn(a, b):
    return lax.dot_general(a, b, (((0,), (0,)), ((), ())), preferred_element_type=F32)


def _split_bf16(v):
    hi = v.astype(BF16)
    lo = (v - hi.astype(F32)).astype(BF16)
    return hi, lo


def _silu(v):
    return v * jax.nn.sigmoid(v)


def _softplus(v):
    return jnp.maximum(v, 0.0) + jnp.log(1.0 + jnp.exp(-jnp.abs(v)))


def _layer_norm(v):
    mu = jnp.mean(v, axis=-1, keepdims=True)
    vc = v - mu
    var = jnp.mean(vc * vc, axis=-1, keepdims=True)
    return vc * lax.rsqrt(var + LN_EPS)


def _params(*sem):
    return pltpu.CompilerParams(dimension_semantics=sem, vmem_limit_bytes=VMEM_LIMIT)


def _ada_kernel(c_ref, w_ref, b_ref, o_ref):
    s_hi, s_lo = _split_bf16(_silu(c_ref[...]))
    w_hi, w_lo = _split_bf16(w_ref[...])
    o_ref[...] = _dot(s_hi, w_hi) + _dot(s_lo, w_hi) + _dot(s_hi, w_lo) + b_ref[...]


def _ada(c, w_ada, b_ada):
    bsz = c.shape[0]
    width = w_ada.shape[1]
    tn = 1024
    return pl.pallas_call(
        _ada_kernel,
        out_shape=jax.ShapeDtypeStruct((bsz, width), F32),
        grid=(width // tn,),
        in_specs=[pl.BlockSpec((bsz, D), lambda j: (0, 0)),
                  pl.BlockSpec((D, tn), lambda j: (0, j)),
                  pl.BlockSpec((1, tn), lambda j: (0, j))],
        out_specs=pl.BlockSpec((bsz, tn), lambda j: (0, j)),
        compiler_params=_params("arbitrary"),
        name="ada",
    )(c, w_ada, b_ada.reshape(1, width))


def _inproj_kernel(x_ref, mod_ref, w_ref, ws_ref, wst_ref, h_ref, sm_ref, smt_ref, u_scr):
    @pl.when(pl.program_id(1) == 0)
    def _():
        mod = mod_ref[0]
        u = _layer_norm(x_ref[...]) * (1.0 + mod[1:2]) + mod[0:1]
        ub = u.astype(BF16)
        u_scr[...] = ub
        sm_ref[...] = _dot(ub, ws_ref[...])
        smt_ref[...] = _dot_nt(wst_ref[...], ub)

    h_ref[...] = _dot(u_scr[...], w_ref[...]).astype(BF16)


def _inproj(x2, mod3, w_main, w_small, w_small_t, seq):
    n = x2.shape[0]
    tm = min(1024, seq)
    tn = 1024
    tiles_per_batch = seq // tm
    return pl.pallas_call(
        _inproj_kernel,
        out_shape=(jax.ShapeDtypeStruct((n, H_WIDTH), BF16),
                   jax.ShapeDtypeStruct((n, LANES), F32),
                   jax.ShapeDtypeStruct((2 * GDN_HEADS, n), F32)),
        grid=(n // tm, H_WIDTH // tn),
        in_specs=[pl.BlockSpec((tm, D), lambda i, j: (i, 0)),
                  pl.BlockSpec((1, 6, D), lambda i, j: (i // tiles_per_batch, 0, 0)),
                  pl.BlockSpec((D, tn), lambda i, j: (0, j)),
                  pl.BlockSpec((D, LANES), lambda i, j: (0, 0)),
                  pl.BlockSpec((2 * GDN_HEADS, D), lambda i, j: (0, 0))],
        out_specs=(pl.BlockSpec((tm, tn), lambda i, j: (i, j)),
                   pl.BlockSpec((tm, LANES), lambda i, j: (i, 0)),
                   pl.BlockSpec((2 * GDN_HEADS, tm), lambda i, j: (0, i))),
        scratch_shapes=[pltpu.VMEM((tm, D), BF16)],
        compiler_params=_params("arbitrary", "arbitrary"),
        name="inproj",
    )(x2, mod3, w_main, w_small, w_small_t)


SWA_TQ = 256
SWA_PREV = 128
SWA_BAND = 192


def _swa_kernel(sink_ref, q_ref, kp_ref, vp_ref, kc_ref, vc_ref, o_ref):
    i = pl.program_id(1)
    kwin = jnp.concatenate([kp_ref[...], kc_ref[...]], axis=0)
    vwin = jnp.concatenate([vp_ref[...], vc_ref[...]], axis=0)
    lo_lane = lax.broadcasted_iota(I32, (1, LANES), 1) < HEAD_DIM
    row_top = lax.broadcasted_iota(I32, (2 * CHUNK, 1), 0) < CHUNK
    key_iota = lax.broadcasted_iota(I32, (1, SWA_BAND), 1)
    zero = jnp.zeros((), BF16)
    for kv in range(KV_HEADS):
        k2 = kwin[:, kv * LANES:(kv + 1) * LANES]
        v2 = vwin[:, kv * LANES:(kv + 1) * LANES]
        k_lo = jnp.where(lo_lane, k2, zero)
        k_hi = jnp.where(lo_lane, zero, k2)
        v_lo = jnp.where(lo_lane, v2, zero)
        v_hi = jnp.where(lo_lane, zero, v2)
        sink_e = jnp.where(row_top, sink_ref[kv * 4 + 0], sink_ref[kv * 4 + 2])
        sink_o = jnp.where(row_top, sink_ref[kv * 4 + 1], sink_ref[kv * 4 + 3])
        for c in range(SWA_TQ // CHUNK):
            rows = slice(c * CHUNK, (c + 1) * CHUNK)
            band = slice(c * CHUNK, c * CHUNK + SWA_BAND)
            base = kv * 2 * LANES
            ql = jnp.concatenate([q_ref[rows, base:base + LANES],
                                  q_ref[rows, base + LANES:base + 2 * LANES]], axis=0)
            valid = (i * SWA_TQ - SWA_PREV + c * CHUNK + key_iota) >= 0

            def probs(k_half, sink):
                s = _dot_nt(ql, k_half[band]) * (HEAD_DIM ** -0.5)
                s = jnp.where(valid, s, NEG_INF)
                m = jnp.maximum(jnp.max(s, axis=-1, keepdims=True), sink)
                p = jnp.exp(s - m)
                den = jnp.sum(p, axis=-1, keepdims=True) + jnp.exp(sink - m)
                return p.astype(BF16), den

            p_e, den_e = probs(k_lo, sink_e)
            p_o, den_o = probs(k_hi, sink_o)
            acc = _dot(p_e, v_lo[band]) / den_e + _dot(p_o, v_hi[band]) / den_o
            o_ref[rows, base:base + LANES] = acc[0:CHUNK].astype(BF16)
            o_ref[rows, base + LANES:base + 2 * LANES] = acc[CHUNK:2 * CHUNK].astype(BF16)


def _swa(hcat, sinks, bsz, seq):
    n = hcat.shape[0]
    tq = SWA_TQ
    nq = seq // tq
    kvw = 2 * SWA_KV_W

    def prev_map(b, i, s):
        return (b * (seq // SWA_PREV) + jnp.maximum(i * (tq // SWA_PREV) - 1, 0), COL_K2 // kvw)

    def prev_map_v(b, i, s):
        return (b * (seq // SWA_PREV) + jnp.maximum(i * (tq // SWA_PREV) - 1, 0), COL_V2 // kvw)

    return pl.pallas_call(
        _swa_kernel,
        out_shape=jax.ShapeDtypeStruct((n, SWA_Q_W), BF16),
        grid_spec=pltpu.PrefetchScalarGridSpec(
            num_scalar_prefetch=1,
            grid=(bsz, nq),
            in_specs=[pl.BlockSpec((tq, SWA_Q_W), lambda b, i, s: (b * nq + i, COL_QA // SWA_Q_W)),
                      pl.BlockSpec((SWA_PREV, kvw), prev_map),
                      pl.BlockSpec((SWA_PREV, kvw), prev_map_v),
                      pl.BlockSpec((tq, kvw), lambda b, i, s: (b * nq + i, COL_K2 // kvw)),
                      pl.BlockSpec((tq, kvw), lambda b, i, s: (b * nq + i, COL_V2 // kvw))],
            out_specs=pl.BlockSpec((tq, SWA_Q_W), lambda b, i, s: (b * nq + i, 0))),
        compiler_params=_params("arbitrary", "arbitrary"),
        name="swa",
    )(sinks, hcat, hcat, hcat, hcat, hcat)


GDN_PREV = 16
GDN_CPB = 4
GDN_ROWS = GDN_CPB * CHUNK


def _mm(p, q):
    return _dot(p.astype(BF16), q.astype(BF16))


def _gdn_kernel(qc_ref, kc_ref, vc_ref, z_ref, qp_ref, kp_ref, vp_ref, cw_ref, sm_ref, smt_ref,
                prow_ref, pcol_ref, nw_ref, o_ref, state, xbuf):
    n = pl.program_id(1)
    heads = range(GDN_HEADS)

    @pl.when(n == 0)
    def _():
        state[...] = jnp.zeros_like(state)

    keep = jnp.where(n == 0, 0.0, 1.0)
    for s, (cur, prev) in enumerate(((qc_ref, qp_ref), (kc_ref, kp_ref), (vc_ref, vp_ref))):
        cols = slice(s * GDN_W, (s + 1) * GDN_W)
        xbuf[0:GDN_PREV, cols] = prev[...].astype(F32) * keep
        xbuf[GDN_PREV:GDN_PREV + GDN_ROWS, cols] = cur[...].astype(F32)
    conv = None
    for tap in range(CONV_WIDTH):
        start = GDN_PREV - (CONV_WIDTH - 1) + tap
        term = xbuf[start:start + GDN_ROWS, :] * cw_ref[tap:tap + 1, :]
        conv = term if conv is None else conv + term
    qkv = _silu(conv)

    ri = lax.broadcasted_iota(I32, (CHUNK, CHUNK), 0)
    ci = lax.broadcasted_iota(I32, (CHUNK, CHUNK), 1)
    incl = ri >= ci
    strict = ri > ci
    eye = jnp.where(ri == ci, 1.0, 0.0).astype(F32)
    blk8 = (ri // 8) == (ci // 8)
    blk16 = (ri // 16) == (ci // 16)
    blk32 = (ri // 32) == (ci // 32)
    levels = (blk16 & ~blk8, blk32 & ~blk16, ~blk32)
    tri_lo = jnp.where(incl, 1.0, 0.0).astype(BF16)
    tri_up = jnp.where(ri <= ci, 1.0, 0.0).astype(BF16)

    staged = []
    for c in range(GDN_CPB):
        rows = slice(c * CHUNK, (c + 1) * CHUNK)
        sm = sm_ref[rows, :]
        beta_all = jax.nn.sigmoid(sm)
        g_all = -jnp.exp(prow_ref[0:1, :]) * _softplus(sm + prow_ref[1:2, :])
        g_hi, g_lo = _split_bf16(g_all)
        cum_all = _dot(tri_lo, g_hi) + _dot(tri_lo, g_lo)
        ecum_all = jnp.exp(cum_all)
        cum_last = cum_all[CHUNK - 1:CHUNK, :]
        kscale_all = jnp.exp(cum_last - cum_all)
        cdec_all = jnp.exp(cum_last)
        smt = smt_ref[c]
        g_row = -jnp.exp(pcol_ref[:, 0:1]) * _softplus(smt[GDN_HEADS:2 * GDN_HEADS, :] + pcol_ref[:, 1:2])
        gr_hi, gr_lo = _split_bf16(g_row)
        cum_row = _dot(gr_hi, tri_up) + _dot(gr_lo, tri_up)

        def col(arr, lane):
            return arr[:, lane:lane + 1]

        qs, ks, vs = [], [], []
        for h in heads:
            qh = qkv[rows, h * GDN_DIM:(h + 1) * GDN_DIM]
            kh = qkv[rows, GDN_W + h * GDN_DIM:GDN_W + (h + 1) * GDN_DIM]
            qs.append(qh * lax.rsqrt(jnp.sum(qh * qh, axis=-1, keepdims=True) + RMS_EPS) * (GDN_DIM ** -0.5))
            ks.append(kh * lax.rsqrt(jnp.sum(kh * kh, axis=-1, keepdims=True) + RMS_EPS))
            vs.append(qkv[rows, 2 * GDN_W + h * GDN_DIM:2 * GDN_W + (h + 1) * GDN_DIM])
        beta = [col(beta_all, h) for h in heads]
        ecum = [col(ecum_all, GDN_HEADS + h) for h in heads]
        dec = []
        for h in heads:
            diff = col(cum_all, GDN_HEADS + h) - cum_row[h:h + 1, :]
            dec.append(jnp.where(incl, jnp.exp(jnp.where(incl, diff, 0.0)), 0.0))
        kb = [ks[h].astype(BF16) for h in heads]
        qkk = [_dot_nt(jnp.concatenate([qs[h].astype(BF16), kb[h]], axis=0), kb[h]) for h in heads]
        qk = [(qkk[h][0:CHUNK] * dec[h]).astype(BF16) for h in heads]
        a_mat = [jnp.where(strict, beta[h] * qkk[h][CHUNK:2 * CHUNK] * dec[h], 0.0) for h in heads]

        a8 = [jnp.where(blk8, a_mat[h], 0.0) for h in heads]
        a8_2 = [_mm(a8[h], a8[h]) for h in heads]
        a8_4 = [_mm(a8_2[h], a8_2[h]) for h in heads]
        t = [_mm(eye - a8[h], eye + a8_2[h]) for h in heads]
        t = [_mm(t[h], eye + a8_4[h]) for h in heads]
        for level in levels:
            inner = [_mm(jnp.where(level, a_mat[h], 0.0), t[h]) for h in heads]
            t = [t[h] - _mm(t[h], inner[h]) for h in heads]

        uw = [_mm(t[h], jnp.concatenate([vs[h] * beta[h], ks[h] * (beta[h] * ecum[h])], axis=1))
              for h in heads]
        wq = [jnp.concatenate([uw[h][:, GDN_DIM:2 * GDN_DIM], qs[h] * ecum[h]], axis=0).astype(BF16)
              for h in heads]
        k_dec = [(ks[h] * col(kscale_all, GDN_HEADS + h)).astype(BF16) for h in heads]
        cdec = [col(cdec_all, GDN_HEADS + h) for h in heads]
        staged.append(([uw[h][:, 0:GDN_DIM] for h in heads], wq, qk, k_dec, cdec))

    s_cur = [state[h] for h in heads]
    for c in range(GDN_CPB):
        rows = slice(c * CHUNK, (c + 1) * CHUNK)
        u_c, wq, qk, k_dec, cdec = staged[c]
        ws = [_dot(wq[h], s_cur[h].astype(BF16)) for h in heads]
        vb = [(u_c[h] - ws[h][0:CHUNK]).astype(BF16) for h in heads]
        o_c = [ws[h][CHUNK:2 * CHUNK] + _dot(qk[h], vb[h]) for h in heads]
        s_cur = [cdec[h] * s_cur[h] + _dot_tn(k_dec[h], vb[h]) for h in heads]
        for h in heads:
            lanes = slice(h * GDN_DIM, (h + 1) * GDN_DIM)
            o_n = o_c[h] * lax.rsqrt(jnp.mean(o_c[h] * o_c[h], axis=-1, keepdims=True) + RMS_EPS)
            o_n = o_n * nw_ref[...] * _silu(z_ref[rows, lanes].astype(F32))
            o_ref[rows, lanes] = o_n.astype(BF16)
    for h in heads:
        state[h] = s_cur[h]


def _gdn(hcat, sm, smt3, conv_w, prow, pcol, norm_w, bsz, seq):
    n = hcat.shape[0]
    steps = seq // GDN_ROWS
    pb = GDN_ROWS // GDN_PREV

    def cur(col):
        return pl.BlockSpec((GDN_ROWS, GDN_W), lambda b, c: (b * steps + c, col // GDN_W))

    def prev(col):
        return pl.BlockSpec((GDN_PREV, GDN_W),
                            lambda b, c: (b * steps * pb + jnp.maximum(c * pb - 1, 0), col // GDN_W))

    return pl.pallas_call(
        _gdn_kernel,
        out_shape=jax.ShapeDtypeStruct((n, GDN_W), BF16),
        grid=(bsz, steps),
        in_specs=[cur(COL_QB), cur(COL_KB), cur(COL_VB), cur(COL_ZB),
                  prev(COL_QB), prev(COL_KB), prev(COL_VB),
                  pl.BlockSpec((CONV_WIDTH, 3 * GDN_W), lambda b, c: (0, 0)),
                  pl.BlockSpec((GDN_ROWS, LANES), lambda b, c: (b * steps + c, 0)),
                  pl.BlockSpec((GDN_CPB, 2 * GDN_HEADS, CHUNK), lambda b, c: (b * steps + c, 0, 0)),
                  pl.BlockSpec((2, LANES), lambda b, c: (0, 0)),
                  pl.BlockSpec((GDN_HEADS, LANES), lambda b, c: (0, 0)),
                  pl.BlockSpec((1, GDN_DIM), lambda b, c: (0, 0))],
        out_specs=pl.BlockSpec((GDN_ROWS, GDN_W), lambda b, c: (b * steps + c, 0)),
        scratch_shapes=[pltpu.VMEM((GDN_HEADS, GDN_DIM, GDN_DIM), F32),
                        pltpu.VMEM((GDN_PREV + GDN_ROWS, 3 * GDN_W), F32)],
        compiler_params=_params("arbitrary", "arbitrary"),
        name="gdn",
    )(hcat, hcat, hcat, hcat, hcat, hcat, hcat, conv_w, sm, smt3, prow, pcol, norm_w)


def _outproj_kernel(oa_ref, ob_ref, ga_ref, gb_ref, x_ref, mod_ref, wa_ref, wb_ref, wo_ref,
                    g1_ref, b1_ref, wr_ref, br_ref,
                    x1_ref, u2_ref, ri_ref, rw_ref, cnt_ref, run):
    i = pl.program_id(0)

    @pl.when(i == 0)
    def _():
        run[...] = jnp.zeros_like(run)

    mod = mod_ref[0]
    pa = _dot(oa_ref[...], wa_ref[...])
    pb = _dot(ob_ref[...], wb_ref[...])
    merged = jax.nn.sigmoid(ga_ref[...].astype(F32)) * pa + jax.nn.sigmoid(gb_ref[...].astype(F32)) * pb
    mix = _dot(merged.astype(BF16), wo_ref[...])
    x1 = _layer_norm(ALPHA * x_ref[...] + mod[2:3] * mix) * g1_ref[...] + b1_ref[...]
    x1_ref[...] = x1
    u2 = _layer_norm(x1) * (1.0 + mod[4:5]) + mod[3:4]
    u2_ref[...] = u2

    u_hi, u_lo = _split_bf16(u2)
    w_hi, w_lo = _split_bf16(wr_ref[...])
    logits = _dot(u_hi, w_hi) + _dot(u_lo, w_hi) + _dot(u_hi, w_lo) + br_ref[...]

    tm = logits.shape[0]
    lane = lax.broadcasted_iota(I32, (1, LANES), 1)
    big = jnp.int32(LANES)
    gmask = lane < N_GROUPS
    lg = jnp.where(gmask, logits, NEG_INF)
    gmax = jnp.max(lg, axis=-1, keepdims=True)
    gidx = jnp.min(jnp.where(lg == gmax, lane, big), axis=-1, keepdims=True)
    p_group = 1.0 / jnp.sum(jnp.exp(lg - gmax), axis=-1, keepdims=True)
    emask = (lane >> 3) == (gidx + ROUTE_EXPERT_LANE // EPG)
    le = jnp.where(emask, logits, NEG_INF)
    m1 = jnp.max(le, axis=-1, keepdims=True)
    i1 = jnp.min(jnp.where(le == m1, lane, big), axis=-1, keepdims=True)
    le2 = jnp.where(lane == i1, NEG_INF, le)
    m2 = jnp.max(le2, axis=-1, keepdims=True)
    i2 = jnp.min(jnp.where(le2 == m2, lane, big), axis=-1, keepdims=True)
    e2_rel = jnp.exp(m2 - m1)
    wgt1 = p_group / (1.0 + e2_rel)
    wgt2 = p_group * e2_rel / (1.0 + e2_rel)
    e1 = i1 - ROUTE_EXPERT_LANE
    e2 = i2 - ROUTE_EXPERT_LANE

    hot1 = lane == e1
    hot2 = lane == e2
    onehot = jnp.where(hot1 | hot2, 1.0, 0.0).astype(F32)
    tr = lax.broadcasted_iota(I32, (tm, tm), 0)
    tc = lax.broadcasted_iota(I32, (tm, tm), 1)
    before = jnp.where(tr > tc, 1.0, 0.0).astype(BF16)
    total = run[...] + _dot(before, onehot.astype(BF16))
    r1 = jnp.sum(jnp.where(hot1, total, 0.0), axis=-1, keepdims=True).astype(I32)
    r2 = jnp.sum(jnp.where(hot2, total, 0.0), axis=-1, keepdims=True).astype(I32)
    run[...] = run[...] + jnp.sum(onehot, axis=0, keepdims=True)
    cnt_ref[...] = run[...]

    ri_ref[...] = jnp.where(lane == 0, e1, jnp.where(lane == 1, e2, jnp.where(lane == 2, r1, r2)))
    rw_ref[...] = jnp.where(lane == 0, wgt1, wgt2)


def _outproj(oa, ob, hcat, x2, mod3, wa, wb, wo, ln_g, ln_b, w_route, b_route, seq):
    n = x2.shape[0]
    tm = 256
    tiles_per_batch = seq // tm
    const = dict(pipeline_mode=pl.Buffered(1))
    return pl.pallas_call(
        _outproj_kernel,
        out_shape=(jax.ShapeDtypeStruct((n, D), F32),
                   jax.ShapeDtypeStruct((n, D), F32),
                   jax.ShapeDtypeStruct((n, LANES), I32),
                   jax.ShapeDtypeStruct((n, LANES), F32),
                   jax.ShapeDtypeStruct((1, LANES), F32)),
        grid=(n // tm,),
        in_specs=[pl.BlockSpec((tm, SWA_Q_W), lambda i: (i, 0)),
                  pl.BlockSpec((tm, GDN_W), lambda i: (i, 0)),
                  pl.BlockSpec((tm, D), lambda i: (i, COL_GA // D)),
                  pl.BlockSpec((tm, D), lambda i: (i, COL_GB // D)),
                  pl.BlockSpec((tm, D), lambda i: (i, 0)),
                  pl.BlockSpec((1, 6, D), lambda i: (i // tiles_per_batch, 0, 0)),
                  pl.BlockSpec((SWA_Q_W, D), lambda i: (0, 0), **const),
                  pl.BlockSpec((GDN_W, D), lambda i: (0, 0), **const),
                  pl.BlockSpec((D, D), lambda i: (0, 0), **const),
                  pl.BlockSpec((1, D), lambda i: (0, 0)),
                  pl.BlockSpec((1, D), lambda i: (0, 0)),
                  pl.BlockSpec((D, LANES), lambda i: (0, 0), **const),
                  pl.BlockSpec((1, LANES), lambda i: (0, 0))],
        out_specs=(pl.BlockSpec((tm, D), lambda i: (i, 0)),
                   pl.BlockSpec((tm, D), lambda i: (i, 0)),
                   pl.BlockSpec((tm, LANES), lambda i: (i, 0)),
                   pl.BlockSpec((tm, LANES), lambda i: (i, 0)),
                   pl.BlockSpec((1, LANES), lambda i: (0, 0))),
        scratch_shapes=[pltpu.VMEM((1, LANES), F32)],
        compiler_params=_params("arbitrary"),
        name="outproj",
    )(oa, ob, hcat, hcat, x2, mod3, wa, wb, wo, ln_g, ln_b, w_route, b_route)


DISPATCH_TM = 256


def _dispatch_kernel(pos_ref, fill_ref, u_ref, xs_ref, zeros, sem, fill_sem):
    base = pl.program_id(0) * (2 * DISPATCH_TM)
    n_tiles = xs_ref.shape[0] // MOE_TILE

    @pl.when(pl.program_id(0) == 0)
    def _():
        zeros[...] = jnp.zeros_like(zeros)

        def fill_copy(t):
            return pltpu.make_async_copy(zeros, xs_ref.at[pl.ds(t * MOE_TILE, MOE_TILE), :], fill_sem)

        def fill_start(t, carry):
            @pl.when(fill_ref[t] == 1)
            def _():
                fill_copy(t).start()
            return carry

        def fill_wait(t, carry):
            @pl.when(fill_ref[t] == 1)
            def _():
                fill_copy(t).wait()
            return carry

        lax.fori_loop(0, n_tiles, fill_start, 0)
        lax.fori_loop(0, n_tiles, fill_wait, 0)

    def row_copy(r, k):
        dst = pos_ref[base + 2 * r + k]
        return pltpu.make_async_copy(u_ref.at[pl.ds(r, 1), :], xs_ref.at[pl.ds(dst, 1), :], sem)

    def issue(r, carry):
        row_copy(r, 0).start()
        row_copy(r, 1).start()
        return carry

    def drain(r, carry):
        row_copy(r, 0).wait()
        row_copy(r, 1).wait()
        return carry

    lax.fori_loop(0, DISPATCH_TM, issue, 0)
    lax.fori_loop(0, DISPATCH_TM, drain, 0)


def _dispatch(pos, tile_fill, u2, n_pad):
    n = u2.shape[0]
    return pl.pallas_call(
        _dispatch_kernel,
        out_shape=jax.ShapeDtypeStruct((n_pad, D), F32),
        grid_spec=pltpu.PrefetchScalarGridSpec(
            num_scalar_prefetch=2,
            grid=(n // DISPATCH_TM,),
            in_specs=[pl.BlockSpec((DISPATCH_TM, D), lambda i, p, f: (i, 0))],
            out_specs=pl.BlockSpec(memory_space=pl.ANY),
            scratch_shapes=[pltpu.VMEM((MOE_TILE, D), F32),
                            pltpu.SemaphoreType.DMA(()),
                            pltpu.SemaphoreType.DMA(())]),
        compiler_params=_params("arbitrary"),
        name="dispatch",
    )(pos, tile_fill, u2)


def _moe_kernel(te_ref, tr_ref, tv_ref, x_ref, wgu_ref, wd_ref, y_ref, wgu_bf, wd_bf):
    i = pl.program_id(0)
    fresh = jnp.logical_or(i == 0, te_ref[i] != te_ref[jnp.maximum(i - 1, 0)])

    @pl.when(jnp.logical_and(fresh, tv_ref[i] == 1))
    def _():
        wgu_bf[...] = wgu_ref[0].astype(BF16)
        wd_bf[...] = wd_ref[0].astype(BF16)

    @pl.when(tv_ref[i] == 1)
    def _():
        gu = _dot(x_ref[...].astype(BF16), wgu_bf[...])
        hid = _silu(gu[:, 0:EXPERT_FF]) * gu[:, EXPERT_FF:2 * EXPERT_FF]
        y_ref[...] = _dot(hid.astype(BF16), wd_bf[...])

    @pl.when(tv_ref[i] == 0)
    def _():
        y_ref[...] = jnp.zeros_like(y_ref)


def _moe(tile_expert, tile_row, tile_valid, xs, w_gate_up, w_down):
    n_pad = xs.shape[0]
    n_tiles = n_pad // MOE_TILE
    return pl.pallas_call(
        _moe_kernel,
        out_shape=jax.ShapeDtypeStruct((n_pad, D), F32),
        grid_spec=pltpu.PrefetchScalarGridSpec(
            num_scalar_prefetch=3,
            grid=(n_tiles,),
            in_specs=[pl.BlockSpec((MOE_TILE, D), lambda i, te, tr, tv: (tr[i], 0)),
                      pl.BlockSpec((1, D, 2 * EXPERT_FF), lambda i, te, tr, tv: (te[i], 0, 0)),
                      pl.BlockSpec((1, EXPERT_FF, D), lambda i, te, tr, tv: (te[i], 0, 0))],
            out_specs=pl.BlockSpec((MOE_TILE, D), lambda i, te, tr, tv: (i, 0)),
            scratch_shapes=[pltpu.VMEM((D, 2 * EXPERT_FF), BF16),
                            pltpu.VMEM((EXPERT_FF, D), BF16)]),
        compiler_params=_params("arbitrary"),
        name="moe",
    )(tile_expert, tile_row, tile_valid, xs, w_gate_up, w_down)


COMBINE_TM = 256


def _combine_kernel(pos_ref, ys_ref, rw_ref, x1_ref, mod_ref, g2_ref, b2_ref, o_ref, buf, sem):
    base = pl.program_id(0) * (2 * COMBINE_TM)

    def row_copy(r, k):
        src = pos_ref[base + 2 * r + k]
        return pltpu.make_async_copy(ys_ref.at[pl.ds(src, 1), :], buf.at[k, pl.ds(r, 1), :], sem)

    def issue(r, carry):
        row_copy(r, 0).start()
        row_copy(r, 1).start()
        return carry

    def drain(r, carry):
        row_copy(r, 0).wait()
        row_copy(r, 1).wait()
        return carry

    lax.fori_loop(0, COMBINE_TM, issue, 0)
    lax.fori_loop(0, COMBINE_TM, drain, 0)

    rw = rw_ref[...]
    ffn = rw[:, 0:1] * buf[0] + rw[:, 1:2] * buf[1]
    mod = mod_ref[0]
    o_ref[...] = _layer_norm(ALPHA * x1_ref[...] + mod[5:6] * ffn) * g2_ref[...] + b2_ref[...]


def _combine(pos, ys, rw, x1, mod3, ln_g, ln_b, seq):
    n = x1.shape[0]
    tm = COMBINE_TM
    tiles_per_batch = seq // tm
    return pl.pallas_call(
        _combine_kernel,
        out_shape=jax.ShapeDtypeStruct((n, D), F32),
        grid_spec=pltpu.PrefetchScalarGridSpec(
            num_scalar_prefetch=1,
            grid=(n // tm,),
            in_specs=[pl.BlockSpec(memory_space=pl.ANY),
                      pl.BlockSpec((tm, LANES), lambda i, p: (i, 0)),
                      pl.BlockSpec((tm, D), lambda i, p: (i, 0)),
                      pl.BlockSpec((1, 6, D), lambda i, p: (i // tiles_per_batch, 0, 0)),
                      pl.BlockSpec((1, D), lambda i, p: (0, 0)),
                      pl.BlockSpec((1, D), lambda i, p: (0, 0))],
            out_specs=pl.BlockSpec((tm, D), lambda i, p: (i, 0)),
            scratch_shapes=[pltpu.VMEM((2, tm, D), F32),
                            pltpu.SemaphoreType.DMA(())]),
        compiler_params=_params("arbitrary"),
        name="combine",
    )(pos, ys, rw, x1, mod3, ln_g, ln_b)


def _in_weights(w_in):
    sizes = (SWA_Q_W, SWA_KV_W, SWA_KV_W, GDN_W, GDN_W, GDN_W, GDN_W, GDN_HEADS, GDN_HEADS, D, D)
    offs = [0]
    for s in sizes:
        offs.append(offs[-1] + s)
    qa, ka, va, qb, kb, vb, zb, bl, al, ga, gb = (w_in[:, offs[k]:offs[k + 1]] for k in range(len(sizes)))

    def dup(w):
        w4 = w.reshape(D, KV_HEADS, 1, HEAD_DIM)
        return jnp.broadcast_to(w4, (D, KV_HEADS, 2, HEAD_DIM)).reshape(D, 2 * SWA_KV_W)

    w_main = jnp.concatenate([qa, dup(ka), dup(va), qb, kb, vb, zb, ga, gb], axis=1).astype(BF16)
    small = jnp.concatenate([bl, al], axis=1)
    w_small = jnp.pad(small, ((0, 0), (0, LANES - 2 * GDN_HEADS))).astype(BF16)
    w_small_t = small.T.astype(BF16)
    return w_main, w_small, w_small_t


def _route_plan(ri, cnt, n_tiles):
    counts = cnt[0, :N_EXPERTS].astype(I32)
    tiles = (counts + MOE_TILE - 1) // MOE_TILE
    tile_end = jnp.cumsum(tiles)
    row_off = (tile_end - tiles) * MOE_TILE
    pos = (row_off[ri[:, 0:2]] + ri[:, 2:4]).reshape(-1)
    used = tile_end[-1]
    t = jnp.arange(n_tiles, dtype=I32)
    t_eff = jnp.minimum(t, used - 1)
    tile_expert = jnp.minimum(jnp.sum(t_eff[:, None] >= tile_end[None, :], axis=1), N_EXPERTS - 1).astype(I32)
    tile_valid = (t < used).astype(I32)
    last_of_expert = jnp.any((t[:, None] + 1 == tile_end[None, :]) & (tiles[None, :] > 0), axis=1)
    tile_fill = jnp.logical_or(last_of_expert, t >= used).astype(I32)
    return pos.astype(I32), tile_expert, t_eff.astype(I32), tile_valid, tile_fill


def kernel(x, c, w_ada, b_ada, w_in, conv_w, swa_sinks, gdn_a_log, gdn_dt_bias, gdn_norm_w, w_proj_a, w_proj_b, w_out, ln1_g, ln1_b, w_router_group, b_router_group, w_router_expert, b_router_expert, w_gate_up, w_down, ln2_g, ln2_b):
    bsz, seq, _ = x.shape
    n = bsz * seq
    nc = seq // CHUNK
    for l in range(DEPTH):
        x2 = x.reshape(n, D)
        mod3 = _ada(c, w_ada[l], b_ada[l]).reshape(bsz, 6, D)
        w_main, w_small, w_small_t = _in_weights(w_in[l])
        hcat, sm, smt = _inproj(x2, mod3, w_main, w_small, w_small_t, seq)
        smt3 = smt.reshape(2 * GDN_HEADS, bsz * nc, CHUNK).transpose(1, 0, 2)

        oa = _swa(hcat, swa_sinks[l], bsz, seq)

        pad_lo = jnp.zeros((GDN_HEADS,), F32)
        prow = jnp.stack([jnp.pad(jnp.concatenate([pad_lo, gdn_a_log[l]]), (0, LANES - 2 * GDN_HEADS)),
                          jnp.pad(jnp.concatenate([pad_lo, gdn_dt_bias[l]]), (0, LANES - 2 * GDN_HEADS))])
        pcol = jnp.pad(jnp.stack([gdn_a_log[l], gdn_dt_bias[l]], axis=1), ((0, 0), (0, LANES - 2)))
        ob = _gdn(hcat, sm, smt3, conv_w[l], prow, pcol, gdn_norm_w[l].reshape(1, GDN_DIM), bsz, seq)

        w_route = jnp.zeros((D, LANES), F32)
        w_route = w_route.at[:, 0:N_GROUPS].set(w_router_group[l])
        w_route = w_route.at[:, ROUTE_EXPERT_LANE:ROUTE_EXPERT_LANE + N_EXPERTS].set(w_router_expert[l])
        b_route = jnp.zeros((1, LANES), F32)
        b_route = b_route.at[0, 0:N_GROUPS].set(b_router_group[l])
        b_route = b_route.at[0, ROUTE_EXPERT_LANE:ROUTE_EXPERT_LANE + N_EXPERTS].set(b_router_expert[l])
        x1, u2, ri, rw, cnt = _outproj(
            oa, ob, hcat, x2, mod3, w_proj_a[l].astype(BF16), w_proj_b[l].astype(BF16),
            w_out[l].astype(BF16), ln1_g[l].reshape(1, D), ln1_b[l].reshape(1, D), w_route, b_route, seq)

        n_tiles = (2 * n) // MOE_TILE + N_EXPERTS
        pos, tile_expert, tile_row, tile_valid, tile_fill = _route_plan(ri, cnt, n_tiles)
        xs = _dispatch(pos, tile_fill, u2, n_tiles * MOE_TILE)
        ys = _moe(tile_expert, tile_row, tile_valid, xs, w_gate_up[l], w_down[l])
        x2 = _combine(pos, ys, rw, x1, mod3, ln2_g[l].reshape(1, D), ln2_b[l].reshape(1, D), seq)
        x = x2.reshape(bsz, seq, D)
    return x
```

```python
import functools

import jax
import jax.numpy as jnp
from jax import lax
from jax.experimental import pallas as pl
from jax.experimental.pallas import tpu as pltpu

F32 = jnp.float32
BF16 = jnp.bfloat16
I32 = jnp.int32

D = 2048
CHUNK = 64
Q_HEADS = 16
KV_HEADS = 4
HEAD_DIM = 64
SWA_Q_W = 1024
SWA_KV_W = 256
GDN_HEADS = 8
GDN_DIM = 128
GDN_W = 1024
CONV_WIDTH = 4
N_GROUPS = 4
EPG = 8
N_EXPERTS = 32
EXPERT_FF = 512
DEPTH = 1
ALPHA = (2 * DEPTH) ** 0.25
LN_EPS = 1e-5
RMS_EPS = 1e-6
NEG_INF = -1e30

SPLIT_FACTOR = 65537.0
LANES = 128
VMEM_LIMIT = 56 * 1024 * 1024

COL_QA = 0
COL_K2 = 1024
COL_V2 = 1536
COL_QB = 2048
COL_KB = 3072
COL_VB = 4096
COL_ZB = 5120
COL_GA = 6144
COL_GB = 8192
H_WIDTH = 10240

ROUTE_EXPERT_LANE = 32
MOE_TILE = 256
OUTPROJ_CB = 512
ROW_TILES = D // LANES


def _store_token_rows(ref, val):
    for g in range(ROW_TILES):
        ref[pl.ds(g, val.shape[0], stride=ROW_TILES), :] = val[:, g * LANES:(g + 1) * LANES]


def _load_token_rows(ref, n_rows):
    return jnp.concatenate([ref[pl.ds(g, n_rows, stride=ROW_TILES), :] for g in range(ROW_TILES)], axis=1)


def _dot(a, b):
    return jnp.dot(a, b, preferred_element_type=F32)


def _dot_nt(a, b):
    return lax.dot_general(a, b, (((1,), (1,)), ((), ())), preferred_element_type=F32)


def _dot_tn(a, b):
    return lax.dot_general(a, b, (((0,), (0,)), ((), ())), preferred_element_type=F32)


def _split_bf16(v):
    c = v * SPLIT_FACTOR
    hi = c - (c - v)
    return hi.astype(BF16), (v - hi).astype(BF16)


def _silu(v):
    return v * jax.nn.sigmoid(v)


def _softplus(v):
    return jnp.maximum(v, 0.0) + jnp.log(1.0 + jnp.exp(-jnp.abs(v)))


def _layer_norm(v):
    mu = jnp.mean(v, axis=-1, keepdims=True)
    vc = v - mu
    var = jnp.mean(vc * vc, axis=-1, keepdims=True)
    return vc * lax.rsqrt(var + LN_EPS)


def _params(*sem):
    return pltpu.CompilerParams(dimension_semantics=sem, vmem_limit_bytes=VMEM_LIMIT)


def _ada_kernel(c_ref, w_ref, b_ref, o_ref):
    s_hi, s_lo = _split_bf16(_silu(c_ref[...]))
    w_hi, w_lo = _split_bf16(w_ref[...])
    o_ref[...] = _dot(s_hi, w_hi) + _dot(s_lo, w_hi) + _dot(s_hi, w_lo) + b_ref[...]


def _ada(c, w_ada, b_ada):
    bsz = c.shape[0]
    width = w_ada.shape[1]
    tn = 1024
    return pl.pallas_call(
        _ada_kernel,
        out_shape=jax.ShapeDtypeStruct((bsz, width), F32),
        grid=(width // tn,),
        in_specs=[pl.BlockSpec((bsz, D), lambda j: (0, 0)),
                  pl.BlockSpec((D, tn), lambda j: (0, j)),
                  pl.BlockSpec((1, tn), lambda j: (0, j))],
        out_specs=pl.BlockSpec((bsz, tn), lambda j: (0, j)),
        compiler_params=_params("arbitrary"),
        name="ada",
    )(c, w_ada, b_ada.reshape(1, width))


def _inproj_kernel(x_ref, mod_ref, w_ref, ws_ref, wst_ref, h_ref, sm_ref, smt_ref, u_scr):
    @pl.when(pl.program_id(1) == 0)
    def _():
        mod = mod_ref[0]
        u = _layer_norm(x_ref[...]) * (1.0 + mod[1:2]) + mod[0:1]
        ub = u.astype(BF16)
        u_scr[...] = ub
        sm_ref[...] = _dot(ub, ws_ref[...])
        smt_ref[...] = _dot_nt(wst_ref[...], ub)

    h_ref[...] = _dot(u_scr[...], w_ref[...]).astype(BF16)


def _inproj(x2, mod3, w_main, w_small, w_small_t, seq):
    n = x2.shape[0]
    tm = min(1024, seq)
    tn = 1024
    tiles_per_batch = seq // tm
    return pl.pallas_call(
        _inproj_kernel,
        out_shape=(jax.ShapeDtypeStruct((n, H_WIDTH), BF16),
                   jax.ShapeDtypeStruct((n, LANES), F32),
                   jax.ShapeDtypeStruct((2 * GDN_HEADS, n), F32)),
        grid=(n // tm, H_WIDTH // tn),
        in_specs=[pl.BlockSpec((tm, D), lambda i, j: (i, 0)),
                  pl.BlockSpec((1, 6, D), lambda i, j: (i // tiles_per_batch, 0, 0)),
                  pl.BlockSpec((D, tn), lambda i, j: (0, j)),
                  pl.BlockSpec((D, LANES), lambda i, j: (0, 0)),
                  pl.BlockSpec((2 * GDN_HEADS, D), lambda i, j: (0, 0))],
        out_specs=(pl.BlockSpec((tm, tn), lambda i, j: (i, j)),
                   pl.BlockSpec((tm, LANES), lambda i, j: (i, 0)),
                   pl.BlockSpec((2 * GDN_HEADS, tm), lambda i, j: (0, i))),
        scratch_shapes=[pltpu.VMEM((tm, D), BF16)],
        compiler_params=_params("arbitrary", "arbitrary"),
        name="inproj",
    )(x2, mod3, w_main, w_small, w_small_t)


SWA_TQ = 256
SWA_PREV = 128
SWA_BAND = 192


def _swa_kernel(sink_ref, q_ref, kp_ref, vp_ref, kc_ref, vc_ref, o_ref):
    i = pl.program_id(1)
    kwin = jnp.concatenate([kp_ref[...], kc_ref[...]], axis=0)
    vwin = jnp.concatenate([vp_ref[...], vc_ref[...]], axis=0)
    lo_lane = lax.broadcasted_iota(I32, (1, LANES), 1) < HEAD_DIM
    row_top = lax.broadcasted_iota(I32, (2 * CHUNK, 1), 0) < CHUNK
    key_iota = lax.broadcasted_iota(I32, (1, SWA_BAND), 1)
    zero = jnp.zeros((), BF16)
    for kv in range(KV_HEADS):
        k2 = kwin[:, kv * LANES:(kv + 1) * LANES]
        v2 = vwin[:, kv * LANES:(kv + 1) * LANES]
        k_lo = jnp.where(lo_lane, k2, zero)
        k_hi = jnp.where(lo_lane, zero, k2)
        v_lo = jnp.where(lo_lane, v2, zero)
        v_hi = jnp.where(lo_lane, zero, v2)
        sinks = (jnp.where(row_top, sink_ref[kv * 4 + 0], sink_ref[kv * 4 + 2]),
                 jnp.where(row_top, sink_ref[kv * 4 + 1], sink_ref[kv * 4 + 3]))
        base = kv * 2 * LANES
        chunks = range(SWA_TQ // CHUNK)
        items = [(c, par) for c in chunks for par in range(2)]
        ql = [jnp.concatenate([q_ref[c * CHUNK:(c + 1) * CHUNK, base:base + LANES],
                               q_ref[c * CHUNK:(c + 1) * CHUNK, base + LANES:base + 2 * LANES]], axis=0)
              for c in chunks]
        scores = [_dot_nt(ql[c], (k_lo, k_hi)[par][c * CHUNK:c * CHUNK + SWA_BAND]) * (HEAD_DIM ** -0.5)
                  for c, par in items]
        scores = [jnp.where((i * SWA_TQ - SWA_PREV + c * CHUNK + key_iota) >= 0, s, NEG_INF)
                  for (c, par), s in zip(items, scores)]
        top = [jnp.maximum(jnp.max(s, axis=-1, keepdims=True), sinks[par]) for (c, par), s in zip(items, scores)]
        probs = [jnp.exp(s - m) for s, m in zip(scores, top)]
        den = [jnp.sum(p, axis=-1, keepdims=True) + jnp.exp(sinks[par] - m)
               for (c, par), p, m in zip(items, probs, top)]
        outs = [_dot(p.astype(BF16), (v_lo, v_hi)[par][c * CHUNK:c * CHUNK + SWA_BAND]) / d
                for (c, par), p, d in zip(items, probs, den)]
        for c in chunks:
            rows = slice(c * CHUNK, (c + 1) * CHUNK)
            acc = outs[2 * c] + outs[2 * c + 1]
            o_ref[rows, base:base + LANES] = acc[0:CHUNK].astype(BF16)
            o_ref[rows, base + LANES:base + 2 * LANES] = acc[CHUNK:2 * CHUNK].astype(BF16)


def _swa(hcat, sinks, bsz, seq):
    n = hcat.shape[0]
    tq = SWA_TQ
    nq = seq // tq
    kvw = 2 * SWA_KV_W

    def prev_map(b, i, s):
        return (b * (seq // SWA_PREV) + jnp.maximum(i * (tq // SWA_PREV) - 1, 0), COL_K2 // kvw)

    def prev_map_v(b, i, s):
        return (b * (seq // SWA_PREV) + jnp.maximum(i * (tq // SWA_PREV) - 1, 0), COL_V2 // kvw)

    return pl.pallas_call(
        _swa_kernel,
        out_shape=jax.ShapeDtypeStruct((n, SWA_Q_W), BF16),
        grid_spec=pltpu.PrefetchScalarGridSpec(
            num_scalar_prefetch=1,
            grid=(bsz, nq),
            in_specs=[pl.BlockSpec((tq, SWA_Q_W), lambda b, i, s: (b * nq + i, COL_QA // SWA_Q_W)),
                      pl.BlockSpec((SWA_PREV, kvw), prev_map),
                      pl.BlockSpec((SWA_PREV, kvw), prev_map_v),
                      pl.BlockSpec((tq, kvw), lambda b, i, s: (b * nq + i, COL_K2 // kvw)),
                      pl.BlockSpec((tq, kvw), lambda b, i, s: (b * nq + i, COL_V2 // kvw))],
            out_specs=pl.BlockSpec((tq, SWA_Q_W), lambda b, i, s: (b * nq + i, 0))),
        compiler_params=_params("arbitrary", "arbitrary"),
        name="swa",
    )(sinks, hcat, hcat, hcat, hcat, hcat)


GDN_PREV = 16
GDN_CPB = 4
GDN_ROWS = GDN_CPB * CHUNK


def _mm(p, q):
    return _dot(p.astype(BF16), q.astype(BF16))


def _gdn_kernel(qc_ref, kc_ref, vc_ref, z_ref, qp_ref, kp_ref, vp_ref, cw_ref, sm_ref, smt_ref,
                prow_ref, pcol_ref, nw_ref, o_ref, state, xbuf):
    n = pl.program_id(1)
    heads = range(GDN_HEADS)

    @pl.when(n == 0)
    def _():
        state[...] = jnp.zeros_like(state)

    keep = jnp.where(n == 0, 0.0, 1.0)
    for s, (cur, prev) in enumerate(((qc_ref, qp_ref), (kc_ref, kp_ref), (vc_ref, vp_ref))):
        cols = slice(s * GDN_W, (s + 1) * GDN_W)
        xbuf[0:GDN_PREV, cols] = prev[...].astype(F32) * keep
        xbuf[GDN_PREV:GDN_PREV + GDN_ROWS, cols] = cur[...].astype(F32)
    conv = None
    for tap in range(CONV_WIDTH):
        start = GDN_PREV - (CONV_WIDTH - 1) + tap
        term = xbuf[start:start + GDN_ROWS, :] * cw_ref[tap:tap + 1, :]
        conv = term if conv is None else conv + term
    qkv = _silu(conv)

    ri = lax.broadcasted_iota(I32, (CHUNK, CHUNK), 0)
    ci = lax.broadcasted_iota(I32, (CHUNK, CHUNK), 1)
    incl = ri >= ci
    strict = ri > ci
    eye = jnp.where(ri == ci, 1.0, 0.0).astype(F32)
    blk8 = (ri // 8) == (ci // 8)
    blk16 = (ri // 16) == (ci // 16)
    blk32 = (ri // 32) == (ci // 32)
    levels = (blk16 & ~blk8, blk32 & ~blk16, ~blk32)
    tri_lo = jnp.where(incl, 1.0, 0.0).astype(BF16)
    tri_up = jnp.where(ri <= ci, 1.0, 0.0).astype(BF16)

    staged = []
    for c in range(GDN_CPB):
        rows = slice(c * CHUNK, (c + 1) * CHUNK)
        sm = sm_ref[rows, :]
        beta_all = jax.nn.sigmoid(sm)
        g_all = -jnp.exp(prow_ref[0:1, :]) * _softplus(sm + prow_ref[1:2, :])
        g_hi, g_lo = _split_bf16(g_all)
        cum_all = _dot(tri_lo, g_hi) + _dot(tri_lo, g_lo)
        ecum_all = jnp.exp(cum_all)
        cum_last = cum_all[CHUNK - 1:CHUNK, :]
        kscale_all = jnp.exp(cum_last - cum_all)
        cdec_all = jnp.exp(cum_last)
        smt = smt_ref[c]
        g_row = -jnp.exp(pcol_ref[:, 0:1]) * _softplus(smt[GDN_HEADS:2 * GDN_HEADS, :] + pcol_ref[:, 1:2])
        gr_hi, gr_lo = _split_bf16(g_row)
        cum_row = _dot(gr_hi, tri_up) + _dot(gr_lo, tri_up)

        def col(arr, lane):
            return arr[:, lane:lane + 1]

        qs, ks, vs = [], [], []
        for h in heads:
            qh = qkv[rows, h * GDN_DIM:(h + 1) * GDN_DIM]
            kh = qkv[rows, GDN_W + h * GDN_DIM:GDN_W + (h + 1) * GDN_DIM]
            qs.append(qh * lax.rsqrt(jnp.sum(qh * qh, axis=-1, keepdims=True) + RMS_EPS) * (GDN_DIM ** -0.5))
            ks.append(kh * lax.rsqrt(jnp.sum(kh * kh, axis=-1, keepdims=True) + RMS_EPS))
            vs.append(qkv[rows, 2 * GDN_W + h * GDN_DIM:2 * GDN_W + (h + 1) * GDN_DIM])
        beta = [col(beta_all, h) for h in heads]
        ecum = [col(ecum_all, GDN_HEADS + h) for h in heads]
        dec = []
        for h in heads:
            diff = col(cum_all, GDN_HEADS + h) - cum_row[h:h + 1, :]
            dec.append(jnp.where(incl, jnp.exp(jnp.where(incl, diff, 0.0)), 0.0))
        kb = [ks[h].astype(BF16) for h in heads]
        qkk = [_dot_nt(jnp.concatenate([qs[h].astype(BF16), kb[h]], axis=0), kb[h]) for h in heads]
        qk = [(qkk[h][0:CHUNK] * dec[h]).astype(BF16) for h in heads]
        a_mat = [jnp.where(strict, beta[h] * qkk[h][CHUNK:2 * CHUNK] * dec[h], 0.0) for h in heads]

        a8 = [jnp.where(blk8, a_mat[h], 0.0) for h in heads]
        a8_2 = [_mm(a8[h], a8[h]) for h in heads]
        a8_4 = [_mm(a8_2[h], a8_2[h]) for h in heads]
        t = [_mm(eye - a8[h], eye + a8_2[h]) for h in heads]
        t = [_mm(t[h], eye + a8_4[h]) for h in heads]
        for level in levels:
            inner = [_mm(jnp.where(level, a_mat[h], 0.0), t[h]) for h in heads]
            t = [t[h] - _mm(t[h], inner[h]) for h in heads]

        uw = [_mm(t[h], jnp.concatenate([vs[h] * beta[h], ks[h] * (beta[h] * ecum[h])], axis=1))
              for h in heads]
        wq = [jnp.concatenate([uw[h][:, GDN_DIM:2 * GDN_DIM], qs[h] * ecum[h]], axis=0).astype(BF16)
              for h in heads]
        k_dec = [(ks[h] * col(kscale_all, GDN_HEADS + h)).astype(BF16) for h in heads]
        cdec = [col(cdec_all, GDN_HEADS + h) for h in heads]
        staged.append(([uw[h][:, 0:GDN_DIM] for h in heads], wq, qk, k_dec, cdec))

    s_cur = [state[h] for h in heads]
    for c in range(GDN_CPB):
        rows = slice(c * CHUNK, (c + 1) * CHUNK)
        u_c, wq, qk, k_dec, cdec = staged[c]
        ws = [_dot(wq[h], s_cur[h].astype(BF16)) for h in heads]
        vb = [(u_c[h] - ws[h][0:CHUNK]).astype(BF16) for h in heads]
        o_c = [ws[h][CHUNK:2 * CHUNK] + _dot(qk[h], vb[h]) for h in heads]
        s_cur = [cdec[h] * s_cur[h] + _dot_tn(k_dec[h], vb[h]) for h in heads]
        for h in heads:
            lanes = slice(h * GDN_DIM, (h + 1) * GDN_DIM)
            o_n = o_c[h] * lax.rsqrt(jnp.mean(o_c[h] * o_c[h], axis=-1, keepdims=True) + RMS_EPS)
            o_n = o_n * nw_ref[...] * _silu(z_ref[rows, lanes].astype(F32))
            o_ref[rows, lanes] = o_n.astype(BF16)
    for h in heads:
        state[h] = s_cur[h]


def _gdn(hcat, sm, smt3, conv_w, prow, pcol, norm_w, bsz, seq):
    n = hcat.shape[0]
    steps = seq // GDN_ROWS
    pb = GDN_ROWS // GDN_PREV

    def cur(col):
        return pl.BlockSpec((GDN_ROWS, GDN_W), lambda b, c: (b * steps + c, col // GDN_W))

    def prev(col):
        return pl.BlockSpec((GDN_PREV, GDN_W),
                            lambda b, c: (b * steps * pb + jnp.maximum(c * pb - 1, 0), col // GDN_W))

    return pl.pallas_call(
        _gdn_kernel,
        out_shape=jax.ShapeDtypeStruct((n, GDN_W), BF16),
        grid=(bsz, steps),
        in_specs=[cur(COL_QB), cur(COL_KB), cur(COL_VB), cur(COL_ZB),
                  prev(COL_QB), prev(COL_KB), prev(COL_VB),
                  pl.BlockSpec((CONV_WIDTH, 3 * GDN_W), lambda b, c: (0, 0)),
                  pl.BlockSpec((GDN_ROWS, LANES), lambda b, c: (b * steps + c, 0)),
                  pl.BlockSpec((GDN_CPB, 2 * GDN_HEADS, CHUNK), lambda b, c: (b * steps + c, 0, 0)),
                  pl.BlockSpec((2, LANES), lambda b, c: (0, 0)),
                  pl.BlockSpec((GDN_HEADS, LANES), lambda b, c: (0, 0)),
                  pl.BlockSpec((1, GDN_DIM), lambda b, c: (0, 0))],
        out_specs=pl.BlockSpec((GDN_ROWS, GDN_W), lambda b, c: (b * steps + c, 0)),
        scratch_shapes=[pltpu.VMEM((GDN_HEADS, GDN_DIM, GDN_DIM), F32),
                        pltpu.VMEM((GDN_PREV + GDN_ROWS, 3 * GDN_W), F32)],
        compiler_params=_params("arbitrary", "arbitrary"),
        name="gdn",
    )(hcat, hcat, hcat, hcat, hcat, hcat, hcat, conv_w, sm, smt3, prow, pcol, norm_w)


def _outproj_kernel(oa_ref, ob_ref, ga_ref, gb_ref, x_ref, mod_ref, wa_ref, wb_ref, wo_ref,
                    g1_ref, b1_ref, wr_ref, br_ref,
                    x1_ref, u2_ref, ri_ref, rw_ref, cnt_ref, run):
    i = pl.program_id(0)

    @pl.when(i == 0)
    def _():
        run[...] = jnp.zeros_like(run)

    mod = mod_ref[0]
    oa = oa_ref[...]
    ob = ob_ref[...]
    mix = None
    for cb in range(D // OUTPROJ_CB):
        cs = slice(cb * OUTPROJ_CB, (cb + 1) * OUTPROJ_CB)
        merged = (jax.nn.sigmoid(ga_ref[:, cs]) * _dot(oa, wa_ref[:, cs]).astype(BF16)
                  + jax.nn.sigmoid(gb_ref[:, cs]) * _dot(ob, wb_ref[:, cs]).astype(BF16))
        part = _dot(merged, wo_ref[cs, :])
        mix = part if mix is None else mix + part
    x1 = _layer_norm(ALPHA * x_ref[...] + mod[2:3] * mix) * g1_ref[...] + b1_ref[...]
    x1_ref[...] = x1
    u2 = _layer_norm(x1) * (1.0 + mod[4:5]) + mod[3:4]
    _store_token_rows(u2_ref, u2)

    u_hi, u_lo = _split_bf16(u2)
    hi_pass = _dot(u_hi, wr_ref[...])
    logits = (hi_pass[:, 0:LANES] + hi_pass[:, LANES:2 * LANES]
              + _dot(u_lo, wr_ref[:, 0:LANES]) + br_ref[...])

    tm = logits.shape[0]
    lane = lax.broadcasted_iota(I32, (1, LANES), 1)
    big = jnp.int32(LANES)
    gmask = lane < N_GROUPS
    lg = jnp.where(gmask, logits, NEG_INF)
    gmax = jnp.max(lg, axis=-1, keepdims=True)
    gidx = jnp.min(jnp.where(lg == gmax, lane, big), axis=-1, keepdims=True)
    p_group = 1.0 / jnp.sum(jnp.exp(lg - gmax), axis=-1, keepdims=True)
    emask = (lane >> 3) == (gidx + ROUTE_EXPERT_LANE // EPG)
    le = jnp.where(emask, logits, NEG_INF)
    m1 = jnp.max(le, axis=-1, keepdims=True)
    i1 = jnp.min(jnp.where(le == m1, lane, big), axis=-1, keepdims=True)
    le2 = jnp.where(lane == i1, NEG_INF, le)
    m2 = jnp.max(le2, axis=-1, keepdims=True)
    i2 = jnp.min(jnp.where(le2 == m2, lane, big), axis=-1, keepdims=True)
    e2_rel = jnp.exp(m2 - m1)
    wgt1 = p_group / (1.0 + e2_rel)
    wgt2 = p_group * e2_rel / (1.0 + e2_rel)
    e1 = i1 - ROUTE_EXPERT_LANE
    e2 = i2 - ROUTE_EXPERT_LANE

    hot1 = lane == e1
    hot2 = lane == e2
    onehot = jnp.where(hot1 | hot2, 1.0, 0.0).astype(F32)
    tr = lax.broadcasted_iota(I32, (tm, tm), 0)
    tc = lax.broadcasted_iota(I32, (tm, tm), 1)
    before = jnp.where(tr > tc, 1.0, 0.0).astype(BF16)
    total = run[...] + _dot(before, onehot.astype(BF16))
    r1 = jnp.sum(jnp.where(hot1, total, 0.0), axis=-1, keepdims=True).astype(I32)
    r2 = jnp.sum(jnp.where(hot2, total, 0.0), axis=-1, keepdims=True).astype(I32)
    run[...] = run[...] + jnp.sum(onehot, axis=0, keepdims=True)
    cnt_ref[...] = run[...]

    ri_ref[...] = jnp.where(lane == 0, e1, jnp.where(lane == 1, e2, jnp.where(lane == 2, r1, r2)))
    rw_ref[...] = jnp.where(lane == 0, wgt1, wgt2)


def _outproj(oa, ob, hcat, x2, mod3, wa, wb, wo, ln_g, ln_b, w_route, b_route, seq):
    n = x2.shape[0]
    tm = 256
    tiles_per_batch = seq // tm
    const = dict(pipeline_mode=pl.Buffered(1))
    return pl.pallas_call(
        _outproj_kernel,
        out_shape=(jax.ShapeDtypeStruct((n, D), F32),
                   jax.ShapeDtypeStruct((n * ROW_TILES, LANES), F32),
                   jax.ShapeDtypeStruct((n, LANES), I32),
                   jax.ShapeDtypeStruct((n, LANES), F32),
                   jax.ShapeDtypeStruct((1, LANES), F32)),
        grid=(n // tm,),
        in_specs=[pl.BlockSpec((tm, SWA_Q_W), lambda i: (i, 0)),
                  pl.BlockSpec((tm, GDN_W), lambda i: (i, 0)),
                  pl.BlockSpec((tm, D), lambda i: (i, COL_GA // D)),
                  pl.BlockSpec((tm, D), lambda i: (i, COL_GB // D)),
                  pl.BlockSpec((tm, D), lambda i: (i, 0)),
                  pl.BlockSpec((1, 6, D), lambda i: (i // tiles_per_batch, 0, 0)),
                  pl.BlockSpec((SWA_Q_W, D), lambda i: (0, 0), **const),
                  pl.BlockSpec((GDN_W, D), lambda i: (0, 0), **const),
                  pl.BlockSpec((D, D), lambda i: (0, 0), **const),
                  pl.BlockSpec((1, D), lambda i: (0, 0)),
                  pl.BlockSpec((1, D), lambda i: (0, 0)),
                  pl.BlockSpec((D, 2 * LANES), lambda i: (0, 0), **const),
                  pl.BlockSpec((1, LANES), lambda i: (0, 0))],
        out_specs=(pl.BlockSpec((tm, D), lambda i: (i, 0)),
                   pl.BlockSpec((tm * ROW_TILES, LANES), lambda i: (i, 0)),
                   pl.BlockSpec((tm, LANES), lambda i: (i, 0)),
                   pl.BlockSpec((tm, LANES), lambda i: (i, 0)),
                   pl.BlockSpec((1, LANES), lambda i: (0, 0))),
        scratch_shapes=[pltpu.VMEM((1, LANES), F32)],
        compiler_params=_params("arbitrary"),
        name="outproj",
    )(oa, ob, hcat, hcat, x2, mod3, wa, wb, wo, ln_g, ln_b, w_route, b_route)


DISPATCH_TM = 256


DMA_UNROLL = 8


def _dispatch_kernel(pos_ref, fill_ref, u_ref, xs_ref, zeros, sem, fill_sem):
    base = pl.program_id(0) * (2 * DISPATCH_TM)
    tile_rows = MOE_TILE * ROW_TILES
    n_tiles = xs_ref.shape[0] // tile_rows

    @pl.when(pl.program_id(0) == 0)
    def _():
        zeros[...] = jnp.zeros_like(zeros)

        def fill_copy(t):
            start = pl.multiple_of(t * tile_rows, tile_rows)
            return pltpu.make_async_copy(zeros, xs_ref.at[pl.ds(start, tile_rows), :], fill_sem)

        def fill_start(t, carry):
            @pl.when(fill_ref[t] == 1)
            def _():
                fill_copy(t).start()
            return carry

        def fill_wait(t, carry):
            @pl.when(fill_ref[t] == 1)
            def _():
                fill_copy(t).wait()
            return carry

        lax.fori_loop(0, n_tiles, fill_start, 0)
        lax.fori_loop(0, n_tiles, fill_wait, 0)

    def row_copy(r, k):
        src = pl.multiple_of(r * ROW_TILES, ROW_TILES)
        dst = pl.multiple_of(pos_ref[base + 2 * r + k], ROW_TILES)
        return pltpu.make_async_copy(u_ref.at[pl.ds(src, ROW_TILES), :],
                                     xs_ref.at[pl.ds(dst, ROW_TILES), :], sem)

    def issue(r, carry):
        row_copy(r, 0).start()
        row_copy(r, 1).start()
        return carry

    def drain(r, carry):
        row_copy(r, 0).wait()
        row_copy(r, 1).wait()
        return carry

    lax.fori_loop(0, DISPATCH_TM, issue, 0, unroll=DMA_UNROLL)
    lax.fori_loop(0, DISPATCH_TM, drain, 0, unroll=DMA_UNROLL)


def _dispatch(pos, tile_fill, u2, n_pad):
    n = u2.shape[0] // ROW_TILES
    return pl.pallas_call(
        _dispatch_kernel,
        out_shape=jax.ShapeDtypeStruct((n_pad * ROW_TILES, LANES), F32),
        grid_spec=pltpu.PrefetchScalarGridSpec(
            num_scalar_prefetch=2,
            grid=(n // DISPATCH_TM,),
            in_specs=[pl.BlockSpec((DISPATCH_TM * ROW_TILES, LANES), lambda i, p, f: (i, 0))],
            out_specs=pl.BlockSpec(memory_space=pl.ANY),
            scratch_shapes=[pltpu.VMEM((MOE_TILE * ROW_TILES, LANES), F32),
                            pltpu.SemaphoreType.DMA(()),
                            pltpu.SemaphoreType.DMA(())]),
        compiler_params=_params("arbitrary"),
        name="dispatch",
    )(pos, tile_fill, u2)


def _moe_kernel(te_ref, tr_ref, tv_ref, x_ref, wgu_ref, wd_ref, y_ref, wgu_bf, wd_bf):
    i = pl.program_id(0)
    fresh = jnp.logical_or(i == 0, te_ref[i] != te_ref[jnp.maximum(i - 1, 0)])

    @pl.when(jnp.logical_and(fresh, tv_ref[i] == 1))
    def _():
        wgu_bf[...] = wgu_ref[0].astype(BF16)
        wd_bf[...] = wd_ref[0].astype(BF16)

    @pl.when(tv_ref[i] == 1)
    def _():
        gu = _dot(_load_token_rows(x_ref, MOE_TILE).astype(BF16), wgu_bf[...])
        hid = _silu(gu[:, 0:EXPERT_FF]) * gu[:, EXPERT_FF:2 * EXPERT_FF]
        _store_token_rows(y_ref, _dot(hid.astype(BF16), wd_bf[...]))

    @pl.when(tv_ref[i] == 0)
    def _():
        y_ref[...] = jnp.zeros_like(y_ref)


def _moe(tile_expert, tile_row, tile_valid, xs, w_gate_up, w_down):
    tile_rows = MOE_TILE * ROW_TILES
    n_tiles = xs.shape[0] // tile_rows
    return pl.pallas_call(
        _moe_kernel,
        out_shape=jax.ShapeDtypeStruct(xs.shape, F32),
        grid_spec=pltpu.PrefetchScalarGridSpec(
            num_scalar_prefetch=3,
            grid=(n_tiles,),
            in_specs=[pl.BlockSpec((tile_rows, LANES), lambda i, te, tr, tv: (tr[i], 0)),
                      pl.BlockSpec((1, D, 2 * EXPERT_FF), lambda i, te, tr, tv: (te[i], 0, 0)),
                      pl.BlockSpec((1, EXPERT_FF, D), lambda i, te, tr, tv: (te[i], 0, 0))],
            out_specs=pl.BlockSpec((tile_rows, LANES), lambda i, te, tr, tv: (i, 0)),
            scratch_shapes=[pltpu.VMEM((D, 2 * EXPERT_FF), BF16),
                            pltpu.VMEM((EXPERT_FF, D), BF16)]),
        compiler_params=_params("arbitrary"),
        name="moe",
    )(tile_expert, tile_row, tile_valid, xs, w_gate_up, w_down)


COMBINE_TM = 256


def _combine_kernel(pos_ref, ys_ref, rw_ref, x1_ref, mod_ref, g2_ref, b2_ref, o_ref, buf, sems):
    i = pl.program_id(0)
    n_steps = pl.num_programs(0)

    def row_copy(step, slot, r, k):
        src = pl.multiple_of(pos_ref[step * (2 * COMBINE_TM) + 2 * r + k], ROW_TILES)
        dst = pl.multiple_of(r * ROW_TILES, ROW_TILES)
        return pltpu.make_async_copy(ys_ref.at[pl.ds(src, ROW_TILES), :],
                                     buf.at[slot, k, pl.ds(dst, ROW_TILES), :], sems.at[slot])

    def issue(step, slot):
        def body(r, carry):
            row_copy(step, slot, r, 0).start()
            row_copy(step, slot, r, 1).start()
            return carry
        lax.fori_loop(0, COMBINE_TM, body, 0, unroll=DMA_UNROLL)

    def drain(step, slot):
        def body(r, carry):
            row_copy(step, slot, r, 0).wait()
            row_copy(step, slot, r, 1).wait()
            return carry
        lax.fori_loop(0, COMBINE_TM, body, 0, unroll=DMA_UNROLL)

    slot = i % 2

    @pl.when(i == 0)
    def _():
        issue(0, 0)

    @pl.when(i + 1 < n_steps)
    def _():
        issue(i + 1, 1 - slot)

    drain(i, slot)

    rw = rw_ref[...]
    y0 = jnp.concatenate([buf[slot, 0, pl.ds(g, COMBINE_TM, stride=ROW_TILES), :] for g in range(ROW_TILES)], axis=1)
    y1 = jnp.concatenate([buf[slot, 1, pl.ds(g, COMBINE_TM, stride=ROW_TILES), :] for g in range(ROW_TILES)], axis=1)
    ffn = rw[:, 0:1] * y0 + rw[:, 1:2] * y1
    mod = mod_ref[0]
    o_ref[...] = _layer_norm(ALPHA * x1_ref[...] + mod[5:6] * ffn) * g2_ref[...] + b2_ref[...]


def _combine(pos, ys, rw, x1, mod3, ln_g, ln_b, seq):
    n = x1.shape[0]
    tm = COMBINE_TM
    tiles_per_batch = seq // tm
    return pl.pallas_call(
        _combine_kernel,
        out_shape=jax.ShapeDtypeStruct((n, D), F32),
        grid_spec=pltpu.PrefetchScalarGridSpec(
            num_scalar_prefetch=1,
            grid=(n // tm,),
            in_specs=[pl.BlockSpec(memory_space=pl.ANY),
                      pl.BlockSpec((tm, LANES), lambda i, p: (i, 0)),
                      pl.BlockSpec((tm, D), lambda i, p: (i, 0)),
                      pl.BlockSpec((1, 6, D), lambda i, p: (i // tiles_per_batch, 0, 0)),
                      pl.BlockSpec((1, D), lambda i, p: (0, 0)),
                      pl.BlockSpec((1, D), lambda i, p: (0, 0))],
            out_specs=pl.BlockSpec((tm, D), lambda i, p: (i, 0)),
            scratch_shapes=[pltpu.VMEM((2, 2, tm * ROW_TILES, LANES), F32),
                            pltpu.SemaphoreType.DMA((2,))]),
        compiler_params=_params("arbitrary"),
        name="combine",
    )(pos, ys, rw, x1, mod3, ln_g, ln_b)


def _in_weights(w_in):
    sizes = (SWA_Q_W, SWA_KV_W, SWA_KV_W, GDN_W, GDN_W, GDN_W, GDN_W, GDN_HEADS, GDN_HEADS, D, D)
    offs = [0]
    for s in sizes:
        offs.append(offs[-1] + s)
    qa, ka, va, qb, kb, vb, zb, bl, al, ga, gb = (w_in[:, offs[k]:offs[k + 1]] for k in range(len(sizes)))

    def dup(w):
        w4 = w.reshape(D, KV_HEADS, 1, HEAD_DIM)
        return jnp.broadcast_to(w4, (D, KV_HEADS, 2, HEAD_DIM)).reshape(D, 2 * SWA_KV_W)

    w_main = jnp.concatenate([qa, dup(ka), dup(va), qb, kb, vb, zb, ga, gb], axis=1).astype(BF16)
    small = jnp.concatenate([bl, al], axis=1)
    w_small = jnp.pad(small, ((0, 0), (0, LANES - 2 * GDN_HEADS))).astype(BF16)
    w_small_t = small.T.astype(BF16)
    return w_main, w_small, w_small_t


def _route_plan(ri, cnt, n_tiles):
    counts = cnt[0, :N_EXPERTS].astype(I32)
    tiles = (counts + MOE_TILE - 1) // MOE_TILE
    tile_end = jnp.cumsum(tiles)
    row_off = (tile_end - tiles) * MOE_TILE
    pos = ((row_off[ri[:, 0:2]] + ri[:, 2:4]) * ROW_TILES).reshape(-1)
    used = tile_end[-1]
    t = jnp.arange(n_tiles, dtype=I32)
    t_eff = jnp.minimum(t, used - 1)
    tile_expert = jnp.minimum(jnp.sum(t_eff[:, None] >= tile_end[None, :], axis=1), N_EXPERTS - 1).astype(I32)
    tile_valid = (t < used).astype(I32)
    last_of_expert = jnp.any((t[:, None] + 1 == tile_end[None, :]) & (tiles[None, :] > 0), axis=1)
    tile_fill = jnp.logical_or(last_of_expert, t >= used).astype(I32)
    return pos.astype(I32), tile_expert, t_eff.astype(I32), tile_valid, tile_fill


def kernel(x, c, w_ada, b_ada, w_in, conv_w, swa_sinks, gdn_a_log, gdn_dt_bias, gdn_norm_w, w_proj_a, w_proj_b, w_out, ln1_g, ln1_b, w_router_group, b_router_group, w_router_expert, b_router_expert, w_gate_up, w_down, ln2_g, ln2_b):
    bsz, seq, _ = x.shape
    n = bsz * seq
    nc = seq // CHUNK
    for l in range(DEPTH):
        x2 = x.reshape(n, D)
        mod3 = _ada(c, w_ada[l], b_ada[l]).reshape(bsz, 6, D)
        w_main, w_small, w_small_t = _in_weights(w_in[l])
        hcat, sm, smt = _inproj(x2, mod3, w_main, w_small, w_small_t, seq)
        smt3 = smt.reshape(2 * GDN_HEADS, bsz * nc, CHUNK).transpose(1, 0, 2)

        oa = _swa(hcat, swa_sinks[l], bsz, seq)

        pad_lo = jnp.zeros((GDN_HEADS,), F32)
        prow = jnp.stack([jnp.pad(jnp.concatenate([pad_lo, gdn_a_log[l]]), (0, LANES - 2 * GDN_HEADS)),
                          jnp.pad(jnp.concatenate([pad_lo, gdn_dt_bias[l]]), (0, LANES - 2 * GDN_HEADS))])
        pcol = jnp.pad(jnp.stack([gdn_a_log[l], gdn_dt_bias[l]], axis=1), ((0, 0), (0, LANES - 2)))
        ob = _gdn(hcat, sm, smt3, conv_w[l], prow, pcol, gdn_norm_w[l].reshape(1, GDN_DIM), bsz, seq)

        w_route = jnp.zeros((D, LANES), F32)
        w_route = w_route.at[:, 0:N_GROUPS].set(w_router_group[l])
        w_route = w_route.at[:, ROUTE_EXPERT_LANE:ROUTE_EXPERT_LANE + N_EXPERTS].set(w_router_expert[l])
        w_route_hi = w_route.astype(BF16)
        w_route = jnp.concatenate([w_route_hi, (w_route - w_route_hi.astype(F32)).astype(BF16)], axis=1)
        b_route = jnp.zeros((1, LANES), F32)
        b_route = b_route.at[0, 0:N_GROUPS].set(b_router_group[l])
        b_route = b_route.at[0, ROUTE_EXPERT_LANE:ROUTE_EXPERT_LANE + N_EXPERTS].set(b_router_expert[l])
        x1, u2, ri, rw, cnt = _outproj(
            oa, ob, hcat, x2, mod3, w_proj_a[l].astype(BF16), w_proj_b[l].astype(BF16),
            w_out[l].astype(BF16), ln1_g[l].reshape(1, D), ln1_b[l].reshape(1, D), w_route, b_route, seq)

        n_tiles = (2 * n) // MOE_TILE + N_EXPERTS
        pos, tile_expert, tile_row, tile_valid, tile_fill = _route_plan(ri, cnt, n_tiles)
        xs = _dispatch(pos, tile_fill, u2, n_tiles * MOE_TILE)
        ys = _moe(tile_expert, tile_row, tile_valid, xs, w_gate_up[l], w_down[l])
        x2 = _combine(pos, ys, rw, x1, mod3, ln2_g[l].reshape(1, D), ln2_b[l].reshape(1, D), seq)
        x = x2.reshape(bsz, seq, D)
    return x
```

```python
import functools

import jax
import jax.numpy as jnp
from jax import lax
from jax.experimental import pallas as pl
from jax.experimental.pallas import tpu as pltpu

F32 = jnp.float32
BF16 = jnp.bfloat16
I32 = jnp.int32

D = 2048
CHUNK = 64
Q_HEADS = 16
KV_HEADS = 4
HEAD_DIM = 64
SWA_Q_W = 1024
SWA_KV_W = 256
GDN_HEADS = 8
GDN_DIM = 128
GDN_W = 1024
CONV_WIDTH = 4
N_GROUPS = 4
EPG = 8
N_EXPERTS = 32
EXPERT_FF = 512
DEPTH = 1
ALPHA = (2 * DEPTH) ** 0.25
LN_EPS = 1e-5
RMS_EPS = 1e-6
NEG_INF = -1e30

SPLIT_FACTOR = 65537.0
LANES = 128
VMEM_LIMIT = 56 * 1024 * 1024

COL_QA = 0
COL_K2 = 1024
COL_V2 = 1536
COL_QB = 2048
COL_KB = 3072
COL_VB = 4096
COL_ZB = 5120
COL_GA = 6144
COL_GB = 8192
H_WIDTH = 10240

ROUTE_EXPERT_LANE = 32
MOE_TILE = 256
OUTPROJ_CB = 512
ROW_TILES = D // LANES


def _store_token_rows(ref, val):
    for g in range(ROW_TILES):
        ref[pl.ds(g, val.shape[0], stride=ROW_TILES), :] = val[:, g * LANES:(g + 1) * LANES]


def _load_token_rows(ref, n_rows):
    return jnp.concatenate([ref[pl.ds(g, n_rows, stride=ROW_TILES), :] for g in range(ROW_TILES)], axis=1)


def _dot(a, b):
    return jnp.dot(a, b, preferred_element_type=F32)


def _dot_nt(a, b):
    return lax.dot_general(a, b, (((1,), (1,)), ((), ())), preferred_element_type=F32)


def _dot_tn(a, b):
    return lax.dot_general(a, b, (((0,), (0,)), ((), ())), preferred_element_type=F32)


def _split_bf16(v):
    c = v * SPLIT_FACTOR
    hi = c - (c - v)
    return hi.astype(BF16), (v - hi).astype(BF16)


def _silu(v):
    return v * jax.nn.sigmoid(v)


def _softplus(v):
    return jnp.maximum(v, 0.0) + jnp.log(1.0 + jnp.exp(-jnp.abs(v)))


def _layer_norm(v):
    mu = jnp.mean(v, axis=-1, keepdims=True)
    vc = v - mu
    var = jnp.mean(vc * vc, axis=-1, keepdims=True)
    return vc * lax.rsqrt(var + LN_EPS)


def _params(*sem):
    return pltpu.CompilerParams(dimension_semantics=sem, vmem_limit_bytes=VMEM_LIMIT)


def _ada_kernel(c_ref, w_ref, b_ref, o_ref):
    s_hi, s_lo = _split_bf16(_silu(c_ref[...]))
    w_hi, w_lo = _split_bf16(w_ref[...])
    o_ref[...] = _dot(s_hi, w_hi) + _dot(s_lo, w_hi) + _dot(s_hi, w_lo) + b_ref[...]


def _ada(c, w_ada, b_ada):
    bsz = c.shape[0]
    width = w_ada.shape[1]
    tn = 1024
    return pl.pallas_call(
        _ada_kernel,
        out_shape=jax.ShapeDtypeStruct((bsz, width), F32),
        grid=(width // tn,),
        in_specs=[pl.BlockSpec((bsz, D), lambda j: (0, 0)),
                  pl.BlockSpec((D, tn), lambda j: (0, j)),
                  pl.BlockSpec((1, tn), lambda j: (0, j))],
        out_specs=pl.BlockSpec((bsz, tn), lambda j: (0, j)),
        compiler_params=_params("arbitrary"),
        name="ada",
    )(c, w_ada, b_ada.reshape(1, width))


def _inproj_kernel(x_ref, mod_ref, w_ref, ws_ref, wst_ref, h_ref, sm_ref, smt_ref, u_scr):
    @pl.when(pl.program_id(1) == 0)
    def _():
        mod = mod_ref[0]
        u = _layer_norm(x_ref[...]) * (1.0 + mod[1:2]) + mod[0:1]
        ub = u.astype(BF16)
        u_scr[...] = ub
        sm_ref[...] = _dot(ub, ws_ref[...])
        smt_ref[...] = _dot_nt(wst_ref[...], ub)

    h_ref[...] = _dot(u_scr[...], w_ref[...]).astype(BF16)


def _inproj(x2, mod3, w_main, w_small, w_small_t, seq):
    n = x2.shape[0]
    tm = min(1024, seq)
    tn = 1024
    tiles_per_batch = seq // tm
    return pl.pallas_call(
        _inproj_kernel,
        out_shape=(jax.ShapeDtypeStruct((n, H_WIDTH), BF16),
                   jax.ShapeDtypeStruct((n, LANES), F32),
                   jax.ShapeDtypeStruct((2 * GDN_HEADS, n), F32)),
        grid=(n // tm, H_WIDTH // tn),
        in_specs=[pl.BlockSpec((tm, D), lambda i, j: (i, 0)),
                  pl.BlockSpec((1, 6, D), lambda i, j: (i // tiles_per_batch, 0, 0)),
                  pl.BlockSpec((D, tn), lambda i, j: (0, j)),
                  pl.BlockSpec((D, LANES), lambda i, j: (0, 0)),
                  pl.BlockSpec((2 * GDN_HEADS, D), lambda i, j: (0, 0))],
        out_specs=(pl.BlockSpec((tm, tn), lambda i, j: (i, j)),
                   pl.BlockSpec((tm, LANES), lambda i, j: (i, 0)),
                   pl.BlockSpec((2 * GDN_HEADS, tm), lambda i, j: (0, i))),
        scratch_shapes=[pltpu.VMEM((tm, D), BF16)],
        compiler_params=_params("arbitrary", "arbitrary"),
        name="inproj",
    )(x2, mod3, w_main, w_small, w_small_t)


SWA_TQ = 256
SWA_PREV = 128
SWA_BAND = 192


def _swa_kernel(sink_ref, q_ref, kp_ref, vp_ref, kc_ref, vc_ref, o_ref):
    i = pl.program_id(1)
    kwin = jnp.concatenate([kp_ref[...], kc_ref[...]], axis=0)
    vwin = jnp.concatenate([vp_ref[...], vc_ref[...]], axis=0)
    lo_lane = lax.broadcasted_iota(I32, (1, LANES), 1) < HEAD_DIM
    row_top = lax.broadcasted_iota(I32, (2 * CHUNK, 1), 0) < CHUNK
    key_iota = lax.broadcasted_iota(I32, (1, SWA_BAND), 1)
    zero = jnp.zeros((), BF16)
    for kv in range(KV_HEADS):
        k2 = kwin[:, kv * LANES:(kv + 1) * LANES]
        v2 = vwin[:, kv * LANES:(kv + 1) * LANES]
        k_lo = jnp.where(lo_lane, k2, zero)
        k_hi = jnp.where(lo_lane, zero, k2)
        v_lo = jnp.where(lo_lane, v2, zero)
        v_hi = jnp.where(lo_lane, zero, v2)
        sinks = (jnp.where(row_top, sink_ref[kv * 4 + 0], sink_ref[kv * 4 + 2]),
                 jnp.where(row_top, sink_ref[kv * 4 + 1], sink_ref[kv * 4 + 3]))
        base = kv * 2 * LANES
        chunks = range(SWA_TQ // CHUNK)
        items = [(c, par) for c in chunks for par in range(2)]
        ql = [jnp.concatenate([q_ref[c * CHUNK:(c + 1) * CHUNK, base:base + LANES],
                               q_ref[c * CHUNK:(c + 1) * CHUNK, base + LANES:base + 2 * LANES]], axis=0)
              for c in chunks]
        scores = [_dot_nt(ql[c], (k_lo, k_hi)[par][c * CHUNK:c * CHUNK + SWA_BAND]) * (HEAD_DIM ** -0.5)
                  for c, par in items]
        scores = [jnp.where((i * SWA_TQ - SWA_PREV + c * CHUNK + key_iota) >= 0, s, NEG_INF)
                  for (c, par), s in zip(items, scores)]
        top = [jnp.maximum(jnp.max(s, axis=-1, keepdims=True), sinks[par]) for (c, par), s in zip(items, scores)]
        probs = [jnp.exp(s - m) for s, m in zip(scores, top)]
        den = [jnp.sum(p, axis=-1, keepdims=True) + jnp.exp(sinks[par] - m)
               for (c, par), p, m in zip(items, probs, top)]
        outs = [_dot(p.astype(BF16), (v_lo, v_hi)[par][c * CHUNK:c * CHUNK + SWA_BAND]) / d
                for (c, par), p, d in zip(items, probs, den)]
        for c in chunks:
            rows = slice(c * CHUNK, (c + 1) * CHUNK)
            acc = outs[2 * c] + outs[2 * c + 1]
            o_ref[rows, base:base + LANES] = acc[0:CHUNK].astype(BF16)
            o_ref[rows, base + LANES:base + 2 * LANES] = acc[CHUNK:2 * CHUNK].astype(BF16)


def _swa(hcat, sinks, bsz, seq):
    n = hcat.shape[0]
    tq = SWA_TQ
    nq = seq // tq
    kvw = 2 * SWA_KV_W

    def prev_map(b, i, s):
        return (b * (seq // SWA_PREV) + jnp.maximum(i * (tq // SWA_PREV) - 1, 0), COL_K2 // kvw)

    def prev_map_v(b, i, s):
        return (b * (seq // SWA_PREV) + jnp.maximum(i * (tq // SWA_PREV) - 1, 0), COL_V2 // kvw)

    return pl.pallas_call(
        _swa_kernel,
        out_shape=jax.ShapeDtypeStruct((n, SWA_Q_W), BF16),
        grid_spec=pltpu.PrefetchScalarGridSpec(
            num_scalar_prefetch=1,
            grid=(bsz, nq),
            in_specs=[pl.BlockSpec((tq, SWA_Q_W), lambda b, i, s: (b * nq + i, COL_QA // SWA_Q_W)),
                      pl.BlockSpec((SWA_PREV, kvw), prev_map),
                      pl.BlockSpec((SWA_PREV, kvw), prev_map_v),
                      pl.BlockSpec((tq, kvw), lambda b, i, s: (b * nq + i, COL_K2 // kvw)),
                      pl.BlockSpec((tq, kvw), lambda b, i, s: (b * nq + i, COL_V2 // kvw))],
            out_specs=pl.BlockSpec((tq, SWA_Q_W), lambda b, i, s: (b * nq + i, 0))),
        compiler_params=_params("arbitrary", "arbitrary"),
        name="swa",
    )(sinks, hcat, hcat, hcat, hcat, hcat)


GDN_PREV = 16
GDN_CPB = 4
GDN_ROWS = GDN_CPB * CHUNK


def _mm(p, q):
    return _dot(p.astype(BF16), q.astype(BF16))


def _gdn_kernel(qc_ref, kc_ref, vc_ref, z_ref, qp_ref, kp_ref, vp_ref, cw_ref, sm_ref, smt_ref,
                prow_ref, pcol_ref, nw_ref, o_ref, state, xbuf):
    n = pl.program_id(1)
    heads = range(GDN_HEADS)

    @pl.when(n == 0)
    def _():
        state[...] = jnp.zeros_like(state)

    keep = jnp.where(n == 0, 0.0, 1.0)
    for s, (cur, prev) in enumerate(((qc_ref, qp_ref), (kc_ref, kp_ref), (vc_ref, vp_ref))):
        cols = slice(s * GDN_W, (s + 1) * GDN_W)
        xbuf[0:GDN_PREV, cols] = prev[...].astype(F32) * keep
        xbuf[GDN_PREV:GDN_PREV + GDN_ROWS, cols] = cur[...].astype(F32)
    conv = None
    for tap in range(CONV_WIDTH):
        start = GDN_PREV - (CONV_WIDTH - 1) + tap
        term = xbuf[start:start + GDN_ROWS, :] * cw_ref[tap:tap + 1, :]
        conv = term if conv is None else conv + term
    qkv = _silu(conv)

    ri = lax.broadcasted_iota(I32, (CHUNK, CHUNK), 0)
    ci = lax.broadcasted_iota(I32, (CHUNK, CHUNK), 1)
    incl = ri >= ci
    strict = ri > ci
    eye = jnp.where(ri == ci, 1.0, 0.0).astype(F32)
    blk8 = (ri // 8) == (ci // 8)
    blk16 = (ri // 16) == (ci // 16)
    blk32 = (ri // 32) == (ci // 32)
    levels = (blk16 & ~blk8, blk32 & ~blk16, ~blk32)
    tri_lo = jnp.where(incl, 1.0, 0.0).astype(BF16)
    tri_up = jnp.where(ri <= ci, 1.0, 0.0).astype(BF16)

    staged = []
    for c in range(GDN_CPB):
        rows = slice(c * CHUNK, (c + 1) * CHUNK)
        sm = sm_ref[rows, :]
        beta_all = jax.nn.sigmoid(sm)
        g_all = -jnp.exp(prow_ref[0:1, :]) * _softplus(sm + prow_ref[1:2, :])
        g_hi, g_lo = _split_bf16(g_all)
        cum_all = _dot(tri_lo, g_hi) + _dot(tri_lo, g_lo)
        ecum_all = jnp.exp(cum_all)
        cum_last = cum_all[CHUNK - 1:CHUNK, :]
        kscale_all = jnp.exp(cum_last - cum_all)
        cdec_all = jnp.exp(cum_last)
        smt = smt_ref[c]
        g_row = -jnp.exp(pcol_ref[:, 0:1]) * _softplus(smt[GDN_HEADS:2 * GDN_HEADS, :] + pcol_ref[:, 1:2])
        gr_hi, gr_lo = _split_bf16(g_row)
        cum_row = _dot(gr_hi, tri_up) + _dot(gr_lo, tri_up)

        def col(arr, lane):
            return arr[:, lane:lane + 1]

        qs, ks, vs = [], [], []
        for h in heads:
            qh = qkv[rows, h * GDN_DIM:(h + 1) * GDN_DIM]
            kh = qkv[rows, GDN_W + h * GDN_DIM:GDN_W + (h + 1) * GDN_DIM]
            qs.append(qh * lax.rsqrt(jnp.sum(qh * qh, axis=-1, keepdims=True) + RMS_EPS) * (GDN_DIM ** -0.5))
            ks.append(kh * lax.rsqrt(jnp.sum(kh * kh, axis=-1, keepdims=True) + RMS_EPS))
            vs.append(qkv[rows, 2 * GDN_W + h * GDN_DIM:2 * GDN_W + (h + 1) * GDN_DIM])
        beta = [col(beta_all, h) for h in heads]
        ecum = [col(ecum_all, GDN_HEADS + h) for h in heads]
        dec = []
        for h in heads:
            diff = col(cum_all, GDN_HEADS + h) - cum_row[h:h + 1, :]
            dec.append(jnp.where(incl, jnp.exp(jnp.where(incl, diff, 0.0)), 0.0))
        kb = [ks[h].astype(BF16) for h in heads]
        qkk = [_dot_nt(jnp.concatenate([qs[h].astype(BF16), kb[h]], axis=0), kb[h]) for h in heads]
        qk = [(qkk[h][0:CHUNK] * dec[h]).astype(BF16) for h in heads]
        a_mat = [jnp.where(strict, beta[h] * qkk[h][CHUNK:2 * CHUNK] * dec[h], 0.0) for h in heads]

        a8 = [jnp.where(blk8, a_mat[h], 0.0) for h in heads]
        a8_2 = [_mm(a8[h], a8[h]) for h in heads]
        a8_4 = [_mm(a8_2[h], a8_2[h]) for h in heads]
        t = [_mm(eye - a8[h], eye + a8_2[h]) for h in heads]
        t = [_mm(t[h], eye + a8_4[h]) for h in heads]
        for level in levels:
            inner = [_mm(jnp.where(level, a_mat[h], 0.0), t[h]) for h in heads]
            t = [t[h] - _mm(t[h], inner[h]) for h in heads]

        uw = [_mm(t[h], jnp.concatenate([vs[h] * beta[h], ks[h] * (beta[h] * ecum[h])], axis=1))
              for h in heads]
        wq = [jnp.concatenate([uw[h][:, GDN_DIM:2 * GDN_DIM], qs[h] * ecum[h]], axis=0).astype(BF16)
              for h in heads]
        k_dec = [(ks[h] * col(kscale_all, GDN_HEADS + h)).astype(BF16) for h in heads]
        cdec = [col(cdec_all, GDN_HEADS + h) for h in heads]
        staged.append(([uw[h][:, 0:GDN_DIM] for h in heads], wq, qk, k_dec, cdec))

    s_cur = [state[h] for h in heads]
    for c in range(GDN_CPB):
        rows = slice(c * CHUNK, (c + 1) * CHUNK)
        u_c, wq, qk, k_dec, cdec = staged[c]
        ws = [_dot(wq[h], s_cur[h].astype(BF16)) for h in heads]
        vb = [(u_c[h] - ws[h][0:CHUNK]).astype(BF16) for h in heads]
        o_c = [ws[h][CHUNK:2 * CHUNK] + _dot(qk[h], vb[h]) for h in heads]
        s_cur = [cdec[h] * s_cur[h] + _dot_tn(k_dec[h], vb[h]) for h in heads]
        for h in heads:
            lanes = slice(h * GDN_DIM, (h + 1) * GDN_DIM)
            o_n = o_c[h] * lax.rsqrt(jnp.mean(o_c[h] * o_c[h], axis=-1, keepdims=True) + RMS_EPS)
            o_n = o_n * nw_ref[...] * _silu(z_ref[rows, lanes].astype(F32))
            o_ref[rows, lanes] = o_n.astype(BF16)
    for h in heads:
        state[h] = s_cur[h]


def _gdn(hcat, sm, smt3, conv_w, prow, pcol, norm_w, bsz, seq):
    n = hcat.shape[0]
    steps = seq // GDN_ROWS
    pb = GDN_ROWS // GDN_PREV

    def cur(col):
        return pl.BlockSpec((GDN_ROWS, GDN_W), lambda b, c: (b * steps + c, col // GDN_W))

    def prev(col):
        return pl.BlockSpec((GDN_PREV, GDN_W),
                            lambda b, c: (b * steps * pb + jnp.maximum(c * pb - 1, 0), col // GDN_W))

    return pl.pallas_call(
        _gdn_kernel,
        out_shape=jax.ShapeDtypeStruct((n, GDN_W), BF16),
        grid=(bsz, steps),
        in_specs=[cur(COL_QB), cur(COL_KB), cur(COL_VB), cur(COL_ZB),
                  prev(COL_QB), prev(COL_KB), prev(COL_VB),
                  pl.BlockSpec((CONV_WIDTH, 3 * GDN_W), lambda b, c: (0, 0)),
                  pl.BlockSpec((GDN_ROWS, LANES), lambda b, c: (b * steps + c, 0)),
                  pl.BlockSpec((GDN_CPB, 2 * GDN_HEADS, CHUNK), lambda b, c: (b * steps + c, 0, 0)),
                  pl.BlockSpec((2, LANES), lambda b, c: (0, 0)),
                  pl.BlockSpec((GDN_HEADS, LANES), lambda b, c: (0, 0)),
                  pl.BlockSpec((1, GDN_DIM), lambda b, c: (0, 0))],
        out_specs=pl.BlockSpec((GDN_ROWS, GDN_W), lambda b, c: (b * steps + c, 0)),
        scratch_shapes=[pltpu.VMEM((GDN_HEADS, GDN_DIM, GDN_DIM), F32),
                        pltpu.VMEM((GDN_PREV + GDN_ROWS, 3 * GDN_W), F32)],
        compiler_params=_params("arbitrary", "arbitrary"),
        name="gdn",
    )(hcat, hcat, hcat, hcat, hcat, hcat, hcat, conv_w, sm, smt3, prow, pcol, norm_w)


OUTPROJ_TM = 256


def _outproj_step(i, oa_ref, ob_ref, ga_ref, gb_ref, x_ref, mod_ref, wa_ref, wb_ref, wo_ref,
                  g1_ref, b1_ref, wr_ref, br_ref, x1_ref, u2_ref, ri_ref, rw_ref, cnt_ref, run,
                  merged_scr, mix_w, mix_r):
    tm = OUTPROJ_TM
    mod = mod_ref[0]
    oa = oa_ref[...]
    ob = ob_ref[...]
    lane = lax.broadcasted_iota(I32, (1, LANES), 1)
    big = jnp.int32(LANES)

    def project(cb):
        cs = slice(cb * OUTPROJ_CB, (cb + 1) * OUTPROJ_CB)
        merged = ((jnp.tanh(ga_ref[:, cs].astype(F32)) + 1.0) * _dot(oa, wa_ref[:, cs])
                  + (jnp.tanh(gb_ref[:, cs].astype(F32)) + 1.0) * _dot(ob, wb_ref[:, cs]))
        merged_scr[:, cs] = merged.astype(BF16)

    def out_project(half):
        cs = slice(half * (D // 2), (half + 1) * (D // 2))
        mix_w[:, cs] = _dot(merged_scr[...], wo_ref[:, cs])

    project(0)
    project(1)

    x1 = _layer_norm(ALPHA * x_ref[...] + (0.5 * mod[2:3]) * mix_r[...]) * g1_ref[...] + b1_ref[...]
    x1_ref[...] = x1

    project(2)
    project(3)

    u2 = _layer_norm(x1) * (1.0 + mod[4:5]) + mod[3:4]
    _store_token_rows(u2_ref, u2)
    u_hi, u_lo = _split_bf16(u2)

    out_project(0)

    hi_pass = _dot(u_hi, wr_ref[...])
    logits = (hi_pass[:, 0:LANES] + hi_pass[:, LANES:2 * LANES]
              + _dot(u_lo, wr_ref[:, 0:LANES]) + br_ref[...])
    gmask = lane < N_GROUPS
    lg = jnp.where(gmask, logits, NEG_INF)
    gmax = jnp.max(lg, axis=-1, keepdims=True)
    gidx = jnp.min(jnp.where(lg == gmax, lane, big), axis=-1, keepdims=True)
    p_group = 1.0 / jnp.sum(jnp.exp(lg - gmax), axis=-1, keepdims=True)
    emask = (lane >> 3) == (gidx + ROUTE_EXPERT_LANE // EPG)
    le = jnp.where(emask, logits, NEG_INF)
    m1 = jnp.max(le, axis=-1, keepdims=True)
    i1 = jnp.min(jnp.where(le == m1, lane, big), axis=-1, keepdims=True)
    le2 = jnp.where(lane == i1, NEG_INF, le)
    m2 = jnp.max(le2, axis=-1, keepdims=True)
    i2 = jnp.min(jnp.where(le2 == m2, lane, big), axis=-1, keepdims=True)
    e2_rel = jnp.exp(m2 - m1)
    wgt1 = p_group / (1.0 + e2_rel)
    wgt2 = p_group * e2_rel / (1.0 + e2_rel)
    e1 = i1 - ROUTE_EXPERT_LANE
    e2 = i2 - ROUTE_EXPERT_LANE

    out_project(1)

    hot1 = lane == e1
    hot2 = lane == e2
    onehot = jnp.where(hot1 | hot2, 1.0, 0.0).astype(F32)
    tr = lax.broadcasted_iota(I32, (tm, tm), 0)
    tc = lax.broadcasted_iota(I32, (tm, tm), 1)
    before = jnp.where(tr > tc, 1.0, 0.0).astype(BF16)
    total = run[...] + _dot(before, onehot.astype(BF16))
    r1 = jnp.sum(jnp.where(hot1, total, 0.0), axis=-1, keepdims=True).astype(I32)
    r2 = jnp.sum(jnp.where(hot2, total, 0.0), axis=-1, keepdims=True).astype(I32)
    live = jnp.where(i == 0, 0.0, 1.0)
    run[...] = (run[...] + jnp.sum(onehot, axis=0, keepdims=True)) * live
    cnt_ref[...] = run[...]
    ri_ref[...] = jnp.where(lane == 0, e1, jnp.where(lane == 1, e2, jnp.where(lane == 2, r1, r2)))
    rw_ref[...] = jnp.where(lane == 0, wgt1, wgt2)


def _outproj_kernel(*refs):
    run, merged_scr, mix_a, mix_b = refs[-4:]
    i = pl.program_id(0)

    @pl.when(i == 0)
    def _():
        run[...] = jnp.zeros_like(run)
        mix_b[...] = jnp.zeros_like(mix_b)

    @pl.when(i % 2 == 0)
    def _():
        _outproj_step(i, *refs[:-2], mix_a, mix_b)

    @pl.when(i % 2 == 1)
    def _():
        _outproj_step(i, *refs[:-2], mix_b, mix_a)


def _outproj(oa, ob, hcat, x2, mod3, wa, wb, wo, ln_g, ln_b, w_route, b_route, seq):
    n = x2.shape[0]
    tm = OUTPROJ_TM
    n_tiles = n // tm
    tiles_per_batch = seq // tm
    const = dict(pipeline_mode=pl.Buffered(1))

    def ahead(i):
        return jnp.minimum(i, n_tiles - 1)

    def behind(i):
        return jnp.maximum(i - 1, 0)

    return pl.pallas_call(
        _outproj_kernel,
        out_shape=(jax.ShapeDtypeStruct((n, D), F32),
                   jax.ShapeDtypeStruct((n * ROW_TILES, LANES), F32),
                   jax.ShapeDtypeStruct((n, LANES), I32),
                   jax.ShapeDtypeStruct((n, LANES), F32),
                   jax.ShapeDtypeStruct((1, LANES), F32)),
        grid=(n_tiles + 1,),
        in_specs=[pl.BlockSpec((tm, SWA_Q_W), lambda i: (ahead(i), 0)),
                  pl.BlockSpec((tm, GDN_W), lambda i: (ahead(i), 0)),
                  pl.BlockSpec((tm, D), lambda i: (ahead(i), COL_GA // D)),
                  pl.BlockSpec((tm, D), lambda i: (ahead(i), COL_GB // D)),
                  pl.BlockSpec((tm, D), lambda i: (behind(i), 0)),
                  pl.BlockSpec((1, 6, D), lambda i: (behind(i) // tiles_per_batch, 0, 0)),
                  pl.BlockSpec((SWA_Q_W, D), lambda i: (0, 0), **const),
                  pl.BlockSpec((GDN_W, D), lambda i: (0, 0), **const),
                  pl.BlockSpec((D, D), lambda i: (0, 0), **const),
                  pl.BlockSpec((1, D), lambda i: (0, 0)),
                  pl.BlockSpec((1, D), lambda i: (0, 0)),
                  pl.BlockSpec((D, 2 * LANES), lambda i: (0, 0), **const),
                  pl.BlockSpec((1, LANES), lambda i: (0, 0))],
        out_specs=(pl.BlockSpec((tm, D), lambda i: (behind(i), 0)),
                   pl.BlockSpec((tm * ROW_TILES, LANES), lambda i: (behind(i), 0)),
                   pl.BlockSpec((tm, LANES), lambda i: (behind(i), 0)),
                   pl.BlockSpec((tm, LANES), lambda i: (behind(i), 0)),
                   pl.BlockSpec((1, LANES), lambda i: (0, 0))),
        scratch_shapes=[pltpu.VMEM((1, LANES), F32),
                        pltpu.VMEM((tm, D), BF16),
                        pltpu.VMEM((tm, D), F32),
                        pltpu.VMEM((tm, D), F32)],
        compiler_params=_params("arbitrary"),
        name="outproj",
    )(oa, ob, hcat, hcat, x2, mod3, wa, wb, wo, ln_g, ln_b, w_route, b_route)


DISPATCH_TM = 256


DMA_UNROLL = 8


def _dispatch_kernel(pos_ref, fill_ref, u_ref, xs_ref, zeros, sem, fill_sem):
    base = pl.program_id(0) * (2 * DISPATCH_TM)
    tile_rows = MOE_TILE * ROW_TILES
    n_tiles = xs_ref.shape[0] // tile_rows

    @pl.when(pl.program_id(0) == 0)
    def _():
        zeros[...] = jnp.zeros_like(zeros)

        def fill_copy(t):
            start = pl.multiple_of(t * tile_rows, tile_rows)
            return pltpu.make_async_copy(zeros, xs_ref.at[pl.ds(start, tile_rows), :], fill_sem)

        def fill_start(t, carry):
            @pl.when(fill_ref[t] == 1)
            def _():
                fill_copy(t).start()
            return carry

        def fill_wait(t, carry):
            @pl.when(fill_ref[t] == 1)
            def _():
                fill_copy(t).wait()
            return carry

        lax.fori_loop(0, n_tiles, fill_start, 0)
        lax.fori_loop(0, n_tiles, fill_wait, 0)

    def row_copy(r, k):
        src = pl.multiple_of(r * ROW_TILES, ROW_TILES)
        dst = pl.multiple_of(pos_ref[base + 2 * r + k], ROW_TILES)
        return pltpu.make_async_copy(u_ref.at[pl.ds(src, ROW_TILES), :],
                                     xs_ref.at[pl.ds(dst, ROW_TILES), :], sem)

    def issue(r, carry):
        row_copy(r, 0).start()
        row_copy(r, 1).start()
        return carry

    def drain(r, carry):
        row_copy(r, 0).wait()
        row_copy(r, 1).wait()
        return carry

    lax.fori_loop(0, DISPATCH_TM, issue, 0, unroll=DMA_UNROLL)
    lax.fori_loop(0, DISPATCH_TM, drain, 0, unroll=DMA_UNROLL)


def _dispatch(pos, tile_fill, u2, n_pad):
    n = u2.shape[0] // ROW_TILES
    return pl.pallas_call(
        _dispatch_kernel,
        out_shape=jax.ShapeDtypeStruct((n_pad * ROW_TILES, LANES), F32),
        grid_spec=pltpu.PrefetchScalarGridSpec(
            num_scalar_prefetch=2,
            grid=(n // DISPATCH_TM,),
            in_specs=[pl.BlockSpec((DISPATCH_TM * ROW_TILES, LANES), lambda i, p, f: (i, 0))],
            out_specs=pl.BlockSpec(memory_space=pl.ANY),
            scratch_shapes=[pltpu.VMEM((MOE_TILE * ROW_TILES, LANES), F32),
                            pltpu.SemaphoreType.DMA(()),
                            pltpu.SemaphoreType.DMA(())]),
        compiler_params=_params("arbitrary"),
        name="dispatch",
    )(pos, tile_fill, u2)


def _moe_kernel(te_ref, tr_ref, tv_ref, x_ref, wgu_ref, wd_ref, y_ref, wgu_bf, wd_bf):
    i = pl.program_id(0)
    fresh = jnp.logical_or(i == 0, te_ref[i] != te_ref[jnp.maximum(i - 1, 0)])

    @pl.when(jnp.logical_and(fresh, tv_ref[i] == 1))
    def _():
        wgu_bf[...] = wgu_ref[0].astype(BF16)
        wd_bf[...] = wd_ref[0].astype(BF16)

    @pl.when(tv_ref[i] == 1)
    def _():
        gu = _dot(_load_token_rows(x_ref, MOE_TILE).astype(BF16), wgu_bf[...])
        hid = _silu(gu[:, 0:EXPERT_FF]) * gu[:, EXPERT_FF:2 * EXPERT_FF]
        _store_token_rows(y_ref, _dot(hid.astype(BF16), wd_bf[...]))

    @pl.when(tv_ref[i] == 0)
    def _():
        y_ref[...] = jnp.zeros_like(y_ref)


def _moe(tile_expert, tile_row, tile_valid, xs, w_gate_up, w_down):
    tile_rows = MOE_TILE * ROW_TILES
    n_tiles = xs.shape[0] // tile_rows
    return pl.pallas_call(
        _moe_kernel,
        out_shape=jax.ShapeDtypeStruct(xs.shape, F32),
        grid_spec=pltpu.PrefetchScalarGridSpec(
            num_scalar_prefetch=3,
            grid=(n_tiles,),
            in_specs=[pl.BlockSpec((tile_rows, LANES), lambda i, te, tr, tv: (tr[i], 0)),
                      pl.BlockSpec((1, D, 2 * EXPERT_FF), lambda i, te, tr, tv: (te[i], 0, 0)),
                      pl.BlockSpec((1, EXPERT_FF, D), lambda i, te, tr, tv: (te[i], 0, 0))],
            out_specs=pl.BlockSpec((tile_rows, LANES), lambda i, te, tr, tv: (i, 0)),
            scratch_shapes=[pltpu.VMEM((D, 2 * EXPERT_FF), BF16),
                            pltpu.VMEM((EXPERT_FF, D), BF16)]),
        compiler_params=_params("arbitrary"),
        name="moe",
    )(tile_expert, tile_row, tile_valid, xs, w_gate_up, w_down)


COMBINE_TM = 256


def _combine_kernel(pos_ref, ys_ref, rw_ref, x1_ref, mod_ref, g2_ref, b2_ref, o_ref, buf, sems):
    i = pl.program_id(0)
    n_steps = pl.num_programs(0)

    def row_copy(step, slot, r, k):
        src = pl.multiple_of(pos_ref[step * (2 * COMBINE_TM) + 2 * r + k], ROW_TILES)
        dst = pl.multiple_of(r * ROW_TILES, ROW_TILES)
        return pltpu.make_async_copy(ys_ref.at[pl.ds(src, ROW_TILES), :],
                                     buf.at[slot, k, pl.ds(dst, ROW_TILES), :], sems.at[slot])

    def issue(step, slot):
        def body(r, carry):
            row_copy(step, slot, r, 0).start()
            row_copy(step, slot, r, 1).start()
            return carry
        lax.fori_loop(0, COMBINE_TM, body, 0, unroll=DMA_UNROLL)

    def drain(step, slot):
        def body(r, carry):
            row_copy(step, slot, r, 0).wait()
            row_copy(step, slot, r, 1).wait()
            return carry
        lax.fori_loop(0, COMBINE_TM, body, 0, unroll=DMA_UNROLL)

    slot = i % 2

    @pl.when(i == 0)
    def _():
        issue(0, 0)

    @pl.when(i + 1 < n_steps)
    def _():
        issue(i + 1, 1 - slot)

    drain(i, slot)

    rw = rw_ref[...]
    y0 = jnp.concatenate([buf[slot, 0, pl.ds(g, COMBINE_TM, stride=ROW_TILES), :] for g in range(ROW_TILES)], axis=1)
    y1 = jnp.concatenate([buf[slot, 1, pl.ds(g, COMBINE_TM, stride=ROW_TILES), :] for g in range(ROW_TILES)], axis=1)
    ffn = rw[:, 0:1] * y0 + rw[:, 1:2] * y1
    mod = mod_ref[0]
    o_ref[...] = _layer_norm(ALPHA * x1_ref[...] + mod[5:6] * ffn) * g2_ref[...] + b2_ref[...]


def _combine(pos, ys, rw, x1, mod3, ln_g, ln_b, seq):
    n = x1.shape[0]
    tm = COMBINE_TM
    tiles_per_batch = seq // tm
    return pl.pallas_call(
        _combine_kernel,
        out_shape=jax.ShapeDtypeStruct((n, D), F32),
        grid_spec=pltpu.PrefetchScalarGridSpec(
            num_scalar_prefetch=1,
            grid=(n // tm,),
            in_specs=[pl.BlockSpec(memory_space=pl.ANY),
                      pl.BlockSpec((tm, LANES), lambda i, p: (i, 0)),
                      pl.BlockSpec((tm, D), lambda i, p: (i, 0)),
                      pl.BlockSpec((1, 6, D), lambda i, p: (i // tiles_per_batch, 0, 0)),
                      pl.BlockSpec((1, D), lambda i, p: (0, 0)),
                      pl.BlockSpec((1, D), lambda i, p: (0, 0))],
            out_specs=pl.BlockSpec((tm, D), lambda i, p: (i, 0)),
            scratch_shapes=[pltpu.VMEM((2, 2, tm * ROW_TILES, LANES), F32),
                            pltpu.SemaphoreType.DMA((2,))]),
        compiler_params=_params("arbitrary"),
        name="combine",
    )(pos, ys, rw, x1, mod3, ln_g, ln_b)


def _in_weights(w_in):
    sizes = (SWA_Q_W, SWA_KV_W, SWA_KV_W, GDN_W, GDN_W, GDN_W, GDN_W, GDN_HEADS, GDN_HEADS, D, D)
    offs = [0]
    for s in sizes:
        offs.append(offs[-1] + s)
    qa, ka, va, qb, kb, vb, zb, bl, al, ga, gb = (w_in[:, offs[k]:offs[k + 1]] for k in range(len(sizes)))

    def dup(w):
        w4 = w.reshape(D, KV_HEADS, 1, HEAD_DIM)
        return jnp.broadcast_to(w4, (D, KV_HEADS, 2, HEAD_DIM)).reshape(D, 2 * SWA_KV_W)

    w_main = jnp.concatenate([qa, dup(ka), dup(va), qb, kb, vb, zb, 0.5 * ga, 0.5 * gb], axis=1).astype(BF16)
    small = jnp.concatenate([bl, al], axis=1)
    w_small = jnp.pad(small, ((0, 0), (0, LANES - 2 * GDN_HEADS))).astype(BF16)
    w_small_t = small.T.astype(BF16)
    return w_main, w_small, w_small_t


def _route_plan(ri, cnt, n_tiles):
    counts = cnt[0, :N_EXPERTS].astype(I32)
    tiles = (counts + MOE_TILE - 1) // MOE_TILE
    tile_end = jnp.cumsum(tiles)
    row_off = (tile_end - tiles) * MOE_TILE
    pos = ((row_off[ri[:, 0:2]] + ri[:, 2:4]) * ROW_TILES).reshape(-1)
    used = tile_end[-1]
    t = jnp.arange(n_tiles, dtype=I32)
    t_eff = jnp.minimum(t, used - 1)
    tile_expert = jnp.minimum(jnp.sum(t_eff[:, None] >= tile_end[None, :], axis=1), N_EXPERTS - 1).astype(I32)
    tile_valid = (t < used).astype(I32)
    last_of_expert = jnp.any((t[:, None] + 1 == tile_end[None, :]) & (tiles[None, :] > 0), axis=1)
    tile_fill = jnp.logical_or(last_of_expert, t >= used).astype(I32)
    return pos.astype(I32), tile_expert, t_eff.astype(I32), tile_valid, tile_fill


def kernel(x, c, w_ada, b_ada, w_in, conv_w, swa_sinks, gdn_a_log, gdn_dt_bias, gdn_norm_w, w_proj_a, w_proj_b, w_out, ln1_g, ln1_b, w_router_group, b_router_group, w_router_expert, b_router_expert, w_gate_up, w_down, ln2_g, ln2_b):
    bsz, seq, _ = x.shape
    n = bsz * seq
    nc = seq // CHUNK
    for l in range(DEPTH):
        x2 = x.reshape(n, D)
        mod3 = _ada(c, w_ada[l], b_ada[l]).reshape(bsz, 6, D)
        w_main, w_small, w_small_t = _in_weights(w_in[l])
        hcat, sm, smt = _inproj(x2, mod3, w_main, w_small, w_small_t, seq)
        smt3 = smt.reshape(2 * GDN_HEADS, bsz * nc, CHUNK).transpose(1, 0, 2)

        oa = _swa(hcat, swa_sinks[l], bsz, seq)

        pad_lo = jnp.zeros((GDN_HEADS,), F32)
        prow = jnp.stack([jnp.pad(jnp.concatenate([pad_lo, gdn_a_log[l]]), (0, LANES - 2 * GDN_HEADS)),
                          jnp.pad(jnp.concatenate([pad_lo, gdn_dt_bias[l]]), (0, LANES - 2 * GDN_HEADS))])
        pcol = jnp.pad(jnp.stack([gdn_a_log[l], gdn_dt_bias[l]], axis=1), ((0, 0), (0, LANES - 2)))
        ob = _gdn(hcat, sm, smt3, conv_w[l], prow, pcol, gdn_norm_w[l].reshape(1, GDN_DIM), bsz, seq)

        w_route = jnp.zeros((D, LANES), F32)
        w_route = w_route.at[:, 0:N_GROUPS].set(w_router_group[l])
        w_route = w_route.at[:, ROUTE_EXPERT_LANE:ROUTE_EXPERT_LANE + N_EXPERTS].set(w_router_expert[l])
        w_route = jnp.concatenate(_split_bf16(w_route), axis=1)
        b_route = jnp.zeros((1, LANES), F32)
        b_route = b_route.at[0, 0:N_GROUPS].set(b_router_group[l])
        b_route = b_route.at[0, ROUTE_EXPERT_LANE:ROUTE_EXPERT_LANE + N_EXPERTS].set(b_router_expert[l])
        x1, u2, ri, rw, cnt = _outproj(
            oa, ob, hcat, x2, mod3, w_proj_a[l].astype(BF16), w_proj_b[l].astype(BF16),
            w_out[l].astype(BF16), ln1_g[l].reshape(1, D), ln1_b[l].reshape(1, D), w_route, b_route, seq)

        n_tiles = (2 * n) // MOE_TILE + N_EXPERTS
        pos, tile_expert, tile_row, tile_valid, tile_fill = _route_plan(ri, cnt, n_tiles)
        xs = _dispatch(pos, tile_fill, u2, n_tiles * MOE_TILE)
        ys = _moe(tile_expert, tile_row, tile_valid, xs, w_gate_up[l], w_down[l])
        x2 = _combine(pos, ys, rw, x1, mod3, ln2_g[l].reshape(1, D), ln2_b[l].reshape(1, D), seq)
        x = x2.reshape(bsz, seq, D)
    return x
```

```python
import functools

import jax
import jax.numpy as jnp
from jax import lax
from jax.experimental import pallas as pl
from jax.experimental.pallas import tpu as pltpu

F32 = jnp.float32
BF16 = jnp.bfloat16
I32 = jnp.int32

D = 2048
CHUNK = 64
Q_HEADS = 16
KV_HEADS = 4
HEAD_DIM = 64
SWA_Q_W = 1024
SWA_KV_W = 256
GDN_HEADS = 8
GDN_DIM = 128
GDN_W = 1024
CONV_WIDTH = 4
N_GROUPS = 4
EPG = 8
N_EXPERTS = 32
EXPERT_FF = 512
DEPTH = 1
ALPHA = (2 * DEPTH) ** 0.25
LN_EPS = 1e-5
RMS_EPS = 1e-6
NEG_INF = -1e30

SPLIT_FACTOR = 65537.0
LANES = 128
VMEM_LIMIT = 56 * 1024 * 1024

COL_QA = 0
COL_K2 = 1024
COL_V2 = 1536
COL_QB = 2048
COL_KB = 3072
COL_VB = 4096
COL_ZB = 5120
COL_GA = 6144
COL_GB = 8192
H_WIDTH = 10240

ROUTE_EXPERT_LANE = 32
MOE_TILE = 256
MOE_FF_BLOCK = 256
MOE_OUT_BLOCK = 512
OUTPROJ_CB = 512
ROW_TILES = D // LANES


def _store_token_rows(ref, val):
    for g in range(ROW_TILES):
        ref[pl.ds(g, val.shape[0], stride=ROW_TILES), :] = val[:, g * LANES:(g + 1) * LANES]


def _load_token_rows(row_group):
    return jnp.concatenate([row_group(g) for g in range(ROW_TILES)], axis=1)


def _dot(a, b):
    return jnp.dot(a, b, preferred_element_type=F32)


def _dot_nt(a, b):
    return lax.dot_general(a, b, (((1,), (1,)), ((), ())), preferred_element_type=F32)


def _dot_tn(a, b):
    return lax.dot_general(a, b, (((0,), (0,)), ((), ())), preferred_element_type=F32)


def _split_bf16(v):
    c = v * SPLIT_FACTOR
    hi = c - (c - v)
    return hi.astype(BF16), (v - hi).astype(BF16)


def _silu(v):
    return v * jax.nn.sigmoid(v)


def _softplus(v):
    return jnp.maximum(v, 0.0) + jnp.log(1.0 + jnp.exp(-jnp.abs(v)))


def _layer_norm(v):
    mu = jnp.mean(v, axis=-1, keepdims=True)
    vc = v - mu
    var = jnp.mean(vc * vc, axis=-1, keepdims=True)
    return vc * lax.rsqrt(var + LN_EPS)


def _params(*sem):
    return pltpu.CompilerParams(dimension_semantics=sem, vmem_limit_bytes=VMEM_LIMIT)


def _ada_kernel(c_ref, w_ref, b_ref, o_ref):
    s_hi, s_lo = _split_bf16(_silu(c_ref[...]))
    w_hi, w_lo = _split_bf16(w_ref[...])
    o_ref[...] = _dot(s_hi, w_hi) + _dot(s_lo, w_hi) + _dot(s_hi, w_lo) + b_ref[...]


def _ada(c, w_ada, b_ada):
    bsz = c.shape[0]
    width = w_ada.shape[1]
    tn = 1024
    return pl.pallas_call(
        _ada_kernel,
        out_shape=jax.ShapeDtypeStruct((bsz, width), F32),
        grid=(width // tn,),
        in_specs=[pl.BlockSpec((bsz, D), lambda j: (0, 0)),
                  pl.BlockSpec((D, tn), lambda j: (0, j)),
                  pl.BlockSpec((1, tn), lambda j: (0, j))],
        out_specs=pl.BlockSpec((bsz, tn), lambda j: (0, j)),
        compiler_params=_params("arbitrary"),
        name="ada",
    )(c, w_ada, b_ada.reshape(1, width))


def _inproj_kernel(x_ref, mod_ref, w_ref, ws_ref, wst_ref, h_ref, sm_ref, smt_ref, u_scr):
    @pl.when(pl.program_id(1) == 0)
    def _():
        mod = mod_ref[0]
        u = _layer_norm(x_ref[...]) * (1.0 + mod[1:2]) + mod[0:1]
        ub = u.astype(BF16)
        u_scr[...] = ub
        sm_ref[...] = _dot(ub, ws_ref[...])
        smt_ref[...] = _dot_nt(wst_ref[...], ub)

    h_ref[...] = _dot_nt(u_scr[...], w_ref[...]).astype(BF16)


def _inproj(x2, mod3, w_main, w_small, w_small_t, seq):
    n = x2.shape[0]
    tm = min(1024, seq)
    tn = 1024
    tiles_per_batch = seq // tm
    return pl.pallas_call(
        _inproj_kernel,
        out_shape=(jax.ShapeDtypeStruct((n, H_WIDTH), BF16),
                   jax.ShapeDtypeStruct((n, LANES), F32),
                   jax.ShapeDtypeStruct((2 * GDN_HEADS, n), F32)),
        grid=(n // tm, H_WIDTH // tn),
        in_specs=[pl.BlockSpec((tm, D), lambda i, j: (i, 0)),
                  pl.BlockSpec((1, 6, D), lambda i, j: (i // tiles_per_batch, 0, 0)),
                  pl.BlockSpec((tn, D), lambda i, j: (j, 0)),
                  pl.BlockSpec((D, LANES), lambda i, j: (0, 0)),
                  pl.BlockSpec((2 * GDN_HEADS, D), lambda i, j: (0, 0))],
        out_specs=(pl.BlockSpec((tm, tn), lambda i, j: (i, j)),
                   pl.BlockSpec((tm, LANES), lambda i, j: (i, 0)),
                   pl.BlockSpec((2 * GDN_HEADS, tm), lambda i, j: (0, i))),
        scratch_shapes=[pltpu.VMEM((tm, D), BF16)],
        compiler_params=_params("arbitrary", "arbitrary"),
        name="inproj",
    )(x2, mod3, w_main, w_small, w_small_t)


SWA_TQ = 256
SWA_PREV = 128
SWA_BAND = 192


def _swa_kernel(sink_ref, q_ref, kp_ref, vp_ref, kc_ref, vc_ref, o_ref):
    i = pl.program_id(1)
    kwin = jnp.concatenate([kp_ref[...], kc_ref[...]], axis=0)
    vwin = jnp.concatenate([vp_ref[...], vc_ref[...]], axis=0)
    lo_lane = lax.broadcasted_iota(I32, (1, LANES), 1) < HEAD_DIM
    row_top = lax.broadcasted_iota(I32, (2 * CHUNK, 1), 0) < CHUNK
    key_iota = lax.broadcasted_iota(I32, (1, SWA_BAND), 1)
    zero = jnp.zeros((), BF16)
    for kv in range(KV_HEADS):
        k2 = kwin[:, kv * LANES:(kv + 1) * LANES]
        v2 = vwin[:, kv * LANES:(kv + 1) * LANES]
        k_lo = jnp.where(lo_lane, k2, zero)
        k_hi = jnp.where(lo_lane, zero, k2)
        v_lo = jnp.where(lo_lane, v2, zero)
        v_hi = jnp.where(lo_lane, zero, v2)
        sinks = (jnp.where(row_top, sink_ref[kv * 4 + 0], sink_ref[kv * 4 + 2]),
                 jnp.where(row_top, sink_ref[kv * 4 + 1], sink_ref[kv * 4 + 3]))
        base = kv * 2 * LANES
        chunks = range(SWA_TQ // CHUNK)
        items = [(c, par) for c in chunks for par in range(2)]
        ql = [jnp.concatenate([q_ref[c * CHUNK:(c + 1) * CHUNK, base:base + LANES],
                               q_ref[c * CHUNK:(c + 1) * CHUNK, base + LANES:base + 2 * LANES]], axis=0)
              for c in chunks]
        scores = [_dot_nt(ql[c], (k_lo, k_hi)[par][c * CHUNK:c * CHUNK + SWA_BAND]) * (HEAD_DIM ** -0.5)
                  for c, par in items]
        scores = [jnp.where((i * SWA_TQ - SWA_PREV + c * CHUNK + key_iota) >= 0, s, NEG_INF)
                  for (c, par), s in zip(items, scores)]
        top = [jnp.maximum(jnp.max(s, axis=-1, keepdims=True), sinks[par]) for (c, par), s in zip(items, scores)]
        probs = [jnp.exp(s - m) for s, m in zip(scores, top)]
        den = [jnp.sum(p, axis=-1, keepdims=True) + jnp.exp(sinks[par] - m)
               for (c, par), p, m in zip(items, probs, top)]
        outs = [_dot(p.astype(BF16), (v_lo, v_hi)[par][c * CHUNK:c * CHUNK + SWA_BAND]) / d
                for (c, par), p, d in zip(items, probs, den)]
        for c in chunks:
            rows = slice(c * CHUNK, (c + 1) * CHUNK)
            acc = outs[2 * c] + outs[2 * c + 1]
            o_ref[rows, base:base + LANES] = acc[0:CHUNK].astype(BF16)
            o_ref[rows, base + LANES:base + 2 * LANES] = acc[CHUNK:2 * CHUNK].astype(BF16)


def _swa(hcat, sinks, bsz, seq):
    n = hcat.shape[0]
    tq = SWA_TQ
    nq = seq // tq
    kvw = 2 * SWA_KV_W

    def prev_map(b, i, s):
        return (b * (seq // SWA_PREV) + jnp.maximum(i * (tq // SWA_PREV) - 1, 0), COL_K2 // kvw)

    def prev_map_v(b, i, s):
        return (b * (seq // SWA_PREV) + jnp.maximum(i * (tq // SWA_PREV) - 1, 0), COL_V2 // kvw)

    return pl.pallas_call(
        _swa_kernel,
        out_shape=jax.ShapeDtypeStruct((n, SWA_Q_W), BF16),
        grid_spec=pltpu.PrefetchScalarGridSpec(
            num_scalar_prefetch=1,
            grid=(bsz, nq),
            in_specs=[pl.BlockSpec((tq, SWA_Q_W), lambda b, i, s: (b * nq + i, COL_QA // SWA_Q_W)),
                      pl.BlockSpec((SWA_PREV, kvw), prev_map),
                      pl.BlockSpec((SWA_PREV, kvw), prev_map_v),
                      pl.BlockSpec((tq, kvw), lambda b, i, s: (b * nq + i, COL_K2 // kvw)),
                      pl.BlockSpec((tq, kvw), lambda b, i, s: (b * nq + i, COL_V2 // kvw))],
            out_specs=pl.BlockSpec((tq, SWA_Q_W), lambda b, i, s: (b * nq + i, 0))),
        compiler_params=_params("arbitrary", "arbitrary"),
        name="swa",
    )(sinks, hcat, hcat, hcat, hcat, hcat)


GDN_PREV = 16
GDN_CPB = 4
GDN_ROWS = GDN_CPB * CHUNK


def _mm(p, q):
    return _dot(p.astype(BF16), q.astype(BF16))


def _gdn_kernel(qc_ref, kc_ref, vc_ref, z_ref, qp_ref, kp_ref, vp_ref, cw_ref, sm_ref, smt_ref,
                prow_ref, pcol_ref, nw_ref, o_ref, state, xbuf):
    n = pl.program_id(1)
    heads = range(GDN_HEADS)

    @pl.when(n == 0)
    def _():
        state[...] = jnp.zeros_like(state)

    keep = jnp.where(n == 0, 0.0, 1.0)
    for s, (cur, prev) in enumerate(((qc_ref, qp_ref), (kc_ref, kp_ref), (vc_ref, vp_ref))):
        cols = slice(s * GDN_W, (s + 1) * GDN_W)
        xbuf[0:GDN_PREV, cols] = prev[...].astype(F32) * keep
        xbuf[GDN_PREV:GDN_PREV + GDN_ROWS, cols] = cur[...].astype(F32)
    conv = None
    for tap in range(CONV_WIDTH):
        start = GDN_PREV - (CONV_WIDTH - 1) + tap
        term = xbuf[start:start + GDN_ROWS, :] * cw_ref[tap:tap + 1, :]
        conv = term if conv is None else conv + term
    qkv = _silu(conv)

    ri = lax.broadcasted_iota(I32, (CHUNK, CHUNK), 0)
    ci = lax.broadcasted_iota(I32, (CHUNK, CHUNK), 1)
    incl = ri >= ci
    strict = ri > ci
    eye = jnp.where(ri == ci, 1.0, 0.0).astype(F32)
    blk8 = (ri // 8) == (ci // 8)
    blk16 = (ri // 16) == (ci // 16)
    blk32 = (ri // 32) == (ci // 32)
    levels = (blk16 & ~blk8, blk32 & ~blk16, ~blk32)
    tri_lo = jnp.where(incl, 1.0, 0.0).astype(BF16)
    tri_up = jnp.where(ri <= ci, 1.0, 0.0).astype(BF16)

    staged = []

    def advance(c, s_old):
        rows = slice(c * CHUNK, (c + 1) * CHUNK)
        u_c, wq, qk, k_dec, cdec = staged[c]
        ws = [_dot(wq[h], s_old[h].astype(BF16)) for h in heads]
        vb = [(u_c[h] - ws[h][0:CHUNK]).astype(BF16) for h in heads]
        o_c = [ws[h][CHUNK:2 * CHUNK] + _dot(qk[h], vb[h]) for h in heads]
        s_new = [cdec[h] * s_old[h] + _dot_tn(k_dec[h], vb[h]) for h in heads]
        for h in heads:
            lanes = slice(h * GDN_DIM, (h + 1) * GDN_DIM)
            o_n = o_c[h] * lax.rsqrt(jnp.mean(o_c[h] * o_c[h], axis=-1, keepdims=True) + RMS_EPS)
            o_n = o_n * nw_ref[...] * _silu(z_ref[rows, lanes].astype(F32))
            o_ref[rows, lanes] = o_n.astype(BF16)
        return s_new

    s_cur = [state[h] for h in heads]
    for c in range(GDN_CPB):
        rows = slice(c * CHUNK, (c + 1) * CHUNK)
        sm = sm_ref[rows, :]
        beta_all = jax.nn.sigmoid(sm)
        g_all = -jnp.exp(prow_ref[0:1, :]) * _softplus(sm + prow_ref[1:2, :])
        g_hi, g_lo = _split_bf16(g_all)
        cum_all = _dot(tri_lo, g_hi) + _dot(tri_lo, g_lo)
        ecum_all = jnp.exp(cum_all)
        cum_last = cum_all[CHUNK - 1:CHUNK, :]
        kscale_all = jnp.exp(cum_last - cum_all)
        cdec_all = jnp.exp(cum_last)
        smt = smt_ref[c]
        g_row = -jnp.exp(pcol_ref[:, 0:1]) * _softplus(smt[GDN_HEADS:2 * GDN_HEADS, :] + pcol_ref[:, 1:2])
        gr_hi, gr_lo = _split_bf16(g_row)
        cum_row = _dot(gr_hi, tri_up) + _dot(gr_lo, tri_up)

        def col(arr, lane):
            return arr[:, lane:lane + 1]

        qs, ks, vs = [], [], []
        for h in heads:
            qh = qkv[rows, h * GDN_DIM:(h + 1) * GDN_DIM]
            kh = qkv[rows, GDN_W + h * GDN_DIM:GDN_W + (h + 1) * GDN_DIM]
            qs.append(qh * lax.rsqrt(jnp.sum(qh * qh, axis=-1, keepdims=True) + RMS_EPS) * (GDN_DIM ** -0.5))
            ks.append(kh * lax.rsqrt(jnp.sum(kh * kh, axis=-1, keepdims=True) + RMS_EPS))
            vs.append(qkv[rows, 2 * GDN_W + h * GDN_DIM:2 * GDN_W + (h + 1) * GDN_DIM])
        beta = [col(beta_all, h) for h in heads]
        ecum = [col(ecum_all, GDN_HEADS + h) for h in heads]
        dec = []
        for h in heads:
            diff = col(cum_all, GDN_HEADS + h) - cum_row[h:h + 1, :]
            dec.append(jnp.where(incl, jnp.exp(jnp.where(incl, diff, 0.0)), 0.0))
        kb = [ks[h].astype(BF16) for h in heads]
        qkk = [_dot_nt(jnp.concatenate([qs[h].astype(BF16), kb[h]], axis=0), kb[h]) for h in heads]
        qk = [(qkk[h][0:CHUNK] * dec[h]).astype(BF16) for h in heads]
        a_mat = [jnp.where(strict, beta[h] * qkk[h][CHUNK:2 * CHUNK] * dec[h], 0.0) for h in heads]

        a8 = [jnp.where(blk8, a_mat[h], 0.0) for h in heads]
        a8_2 = [_mm(a8[h], a8[h]) for h in heads]
        a8_4 = [_mm(a8_2[h], a8_2[h]) for h in heads]
        t = [_mm(eye - a8[h], eye + a8_2[h]) for h in heads]
        t = [_mm(t[h], eye + a8_4[h]) for h in heads]
        for level in levels:
            inner = [_mm(jnp.where(level, a_mat[h], 0.0), t[h]) for h in heads]
            t = [t[h] - _mm(t[h], inner[h]) for h in heads]

        uw = [_mm(t[h], jnp.concatenate([vs[h] * beta[h], ks[h] * (beta[h] * ecum[h])], axis=1))
              for h in heads]
        wq = [jnp.concatenate([uw[h][:, GDN_DIM:2 * GDN_DIM], qs[h] * ecum[h]], axis=0).astype(BF16)
              for h in heads]
        k_dec = [(ks[h] * col(kscale_all, GDN_HEADS + h)).astype(BF16) for h in heads]
        cdec = [col(cdec_all, GDN_HEADS + h) for h in heads]
        staged.append(([uw[h][:, 0:GDN_DIM] for h in heads], wq, qk, k_dec, cdec))

    for c in range(GDN_CPB):
        s_cur = advance(c, s_cur)
    for h in heads:
        state[h] = s_cur[h]


def _gdn(hcat, sm, smt3, conv_w, prow, pcol, norm_w, bsz, seq):
    n = hcat.shape[0]
    steps = seq // GDN_ROWS
    pb = GDN_ROWS // GDN_PREV

    def cur(col):
        return pl.BlockSpec((GDN_ROWS, GDN_W), lambda b, c: (b * steps + c, col // GDN_W))

    def prev(col):
        return pl.BlockSpec((GDN_PREV, GDN_W),
                            lambda b, c: (b * steps * pb + jnp.maximum(c * pb - 1, 0), col // GDN_W))

    return pl.pallas_call(
        _gdn_kernel,
        out_shape=jax.ShapeDtypeStruct((n, GDN_W), BF16),
        grid=(bsz, steps),
        in_specs=[cur(COL_QB), cur(COL_KB), cur(COL_VB), cur(COL_ZB),
                  prev(COL_QB), prev(COL_KB), prev(COL_VB),
                  pl.BlockSpec((CONV_WIDTH, 3 * GDN_W), lambda b, c: (0, 0)),
                  pl.BlockSpec((GDN_ROWS, LANES), lambda b, c: (b * steps + c, 0)),
                  pl.BlockSpec((GDN_CPB, 2 * GDN_HEADS, CHUNK), lambda b, c: (b * steps + c, 0, 0)),
                  pl.BlockSpec((2, LANES), lambda b, c: (0, 0)),
                  pl.BlockSpec((GDN_HEADS, LANES), lambda b, c: (0, 0)),
                  pl.BlockSpec((1, GDN_DIM), lambda b, c: (0, 0))],
        out_specs=pl.BlockSpec((GDN_ROWS, GDN_W), lambda b, c: (b * steps + c, 0)),
        scratch_shapes=[pltpu.VMEM((GDN_HEADS, GDN_DIM, GDN_DIM), F32),
                        pltpu.VMEM((GDN_PREV + GDN_ROWS, 3 * GDN_W), F32)],
        compiler_params=_params("arbitrary", "arbitrary"),
        name="gdn",
    )(hcat, hcat, hcat, hcat, hcat, hcat, hcat, conv_w, sm, smt3, prow, pcol, norm_w)


OUTPROJ_TM = 256


def _outproj_step(i, oa_ref, ob_ref, ga_ref, gb_ref, x_ref, mod_ref, wa_ref, wb_ref, wo_ref,
                  g1_ref, b1_ref, wr_ref, br_ref, x1_ref, u2_ref, ri_ref, rw_ref, cnt_ref, run,
                  merged_scr, mix_w, mix_r):
    tm = OUTPROJ_TM
    mod = mod_ref[0]
    oa = oa_ref[...]
    ob = ob_ref[...]
    lane = lax.broadcasted_iota(I32, (1, LANES), 1)
    big = jnp.int32(LANES)

    def project(cb):
        cs = slice(cb * OUTPROJ_CB, (cb + 1) * OUTPROJ_CB)
        merged = ((jnp.tanh(ga_ref[:, cs].astype(F32)) + 1.0) * _dot(oa, wa_ref[:, cs])
                  + (jnp.tanh(gb_ref[:, cs].astype(F32)) + 1.0) * _dot(ob, wb_ref[:, cs]))
        merged_scr[:, cs] = merged.astype(BF16)

    def out_project(half):
        cs = slice(half * (D // 2), (half + 1) * (D // 2))
        mix_w[:, cs] = _dot(merged_scr[...], wo_ref[:, cs])

    project(0)
    project(1)

    x1 = _layer_norm(ALPHA * x_ref[...] + (0.5 * mod[2:3]) * mix_r[...]) * g1_ref[...] + b1_ref[...]
    x1_ref[...] = x1

    project(2)
    project(3)

    u2 = _layer_norm(x1) * (1.0 + mod[4:5]) + mod[3:4]
    _store_token_rows(u2_ref, u2)
    u_hi, u_lo = _split_bf16(u2)

    out_project(0)

    hi_pass = _dot(u_hi, wr_ref[...])
    logits = (hi_pass[:, 0:LANES] + hi_pass[:, LANES:2 * LANES]
              + _dot(u_lo, wr_ref[:, 0:LANES]) + br_ref[...])
    gmask = lane < N_GROUPS
    lg = jnp.where(gmask, logits, NEG_INF)
    gmax = jnp.max(lg, axis=-1, keepdims=True)
    gidx = jnp.min(jnp.where(lg == gmax, lane, big), axis=-1, keepdims=True)
    p_group = 1.0 / jnp.sum(jnp.exp(lg - gmax), axis=-1, keepdims=True)
    emask = (lane >> 3) == (gidx + ROUTE_EXPERT_LANE // EPG)
    le = jnp.where(emask, logits, NEG_INF)
    m1 = jnp.max(le, axis=-1, keepdims=True)
    i1 = jnp.min(jnp.where(le == m1, lane, big), axis=-1, keepdims=True)
    le2 = jnp.where(lane == i1, NEG_INF, le)
    m2 = jnp.max(le2, axis=-1, keepdims=True)
    i2 = jnp.min(jnp.where(le2 == m2, lane, big), axis=-1, keepdims=True)
    e2_rel = jnp.exp(m2 - m1)
    wgt1 = p_group / (1.0 + e2_rel)
    wgt2 = p_group * e2_rel / (1.0 + e2_rel)
    e1 = i1 - ROUTE_EXPERT_LANE
    e2 = i2 - ROUTE_EXPERT_LANE

    out_project(1)

    hot1 = lane == e1
    hot2 = lane == e2
    onehot = jnp.where(hot1 | hot2, 1.0, 0.0).astype(F32)
    tr = lax.broadcasted_iota(I32, (tm, tm), 0)
    tc = lax.broadcasted_iota(I32, (tm, tm), 1)
    before = jnp.where(tr > tc, 1.0, 0.0).astype(BF16)
    total = run[...] + _dot(before, onehot.astype(BF16))
    r1 = jnp.sum(jnp.where(hot1, total, 0.0), axis=-1, keepdims=True).astype(I32)
    r2 = jnp.sum(jnp.where(hot2, total, 0.0), axis=-1, keepdims=True).astype(I32)
    live = jnp.where(i == 0, 0.0, 1.0)
    run[...] = (run[...] + jnp.sum(onehot, axis=0, keepdims=True)) * live
    cnt_ref[...] = run[...]
    ri_ref[...] = jnp.where(lane == 0, e1, jnp.where(lane == 1, e2, jnp.where(lane == 2, r1, r2)))
    rw_ref[...] = jnp.where(lane == 0, wgt1, wgt2)


def _outproj_kernel(*refs):
    run, merged_scr, mix_a, mix_b = refs[-4:]
    i = pl.program_id(0)

    @pl.when(i == 0)
    def _():
        run[...] = jnp.zeros_like(run)
        mix_b[...] = jnp.zeros_like(mix_b)

    @pl.when(i % 2 == 0)
    def _():
        _outproj_step(i, *refs[:-2], mix_a, mix_b)

    @pl.when(i % 2 == 1)
    def _():
        _outproj_step(i, *refs[:-2], mix_b, mix_a)


def _outproj(oa, ob, hcat, x2, mod3, wa, wb, wo, ln_g, ln_b, w_route, b_route, seq):
    n = x2.shape[0]
    tm = OUTPROJ_TM
    n_tiles = n // tm
    tiles_per_batch = seq // tm
    const = dict(pipeline_mode=pl.Buffered(1))

    def ahead(i):
        return jnp.minimum(i, n_tiles - 1)

    def behind(i):
        return jnp.maximum(i - 1, 0)

    return pl.pallas_call(
        _outproj_kernel,
        out_shape=(jax.ShapeDtypeStruct((n, D), F32),
                   jax.ShapeDtypeStruct((n * ROW_TILES, LANES), F32),
                   jax.ShapeDtypeStruct((n, LANES), I32),
                   jax.ShapeDtypeStruct((n, LANES), F32),
                   jax.ShapeDtypeStruct((1, LANES), F32)),
        grid=(n_tiles + 1,),
        in_specs=[pl.BlockSpec((tm, SWA_Q_W), lambda i: (ahead(i), 0)),
                  pl.BlockSpec((tm, GDN_W), lambda i: (ahead(i), 0)),
                  pl.BlockSpec((tm, D), lambda i: (ahead(i), COL_GA // D)),
                  pl.BlockSpec((tm, D), lambda i: (ahead(i), COL_GB // D)),
                  pl.BlockSpec((tm, D), lambda i: (behind(i), 0)),
                  pl.BlockSpec((1, 6, D), lambda i: (behind(i) // tiles_per_batch, 0, 0)),
                  pl.BlockSpec((SWA_Q_W, D), lambda i: (0, 0), **const),
                  pl.BlockSpec((GDN_W, D), lambda i: (0, 0), **const),
                  pl.BlockSpec((D, D), lambda i: (0, 0), **const),
                  pl.BlockSpec((1, D), lambda i: (0, 0)),
                  pl.BlockSpec((1, D), lambda i: (0, 0)),
                  pl.BlockSpec((D, 2 * LANES), lambda i: (0, 0), **const),
                  pl.BlockSpec((1, LANES), lambda i: (0, 0))],
        out_specs=(pl.BlockSpec((tm, D), lambda i: (behind(i), 0)),
                   pl.BlockSpec((tm * ROW_TILES, LANES), lambda i: (behind(i), 0)),
                   pl.BlockSpec((tm, LANES), lambda i: (behind(i), 0)),
                   pl.BlockSpec((tm, LANES), lambda i: (behind(i), 0)),
                   pl.BlockSpec((1, LANES), lambda i: (0, 0))),
        scratch_shapes=[pltpu.VMEM((1, LANES), F32),
                        pltpu.VMEM((tm, D), BF16),
                        pltpu.VMEM((tm, D), F32),
                        pltpu.VMEM((tm, D), F32)],
        compiler_params=_params("arbitrary"),
        name="outproj",
    )(oa, ob, hcat, hcat, x2, mod3, wa, wb, wo, ln_g, ln_b, w_route, b_route)


DISPATCH_TM = 512


DMA_UNROLL = 8


def _dispatch_kernel(pos_ref, fill_ref, u_ref, xs_ref, zeros, sem, fill_sem):
    base = pl.program_id(0) * (2 * DISPATCH_TM)
    tile_rows = MOE_TILE * ROW_TILES
    n_tiles = xs_ref.shape[0] // tile_rows

    @pl.when(pl.program_id(0) == 0)
    def _():
        zeros[...] = jnp.zeros_like(zeros)

        def fill_copy(t):
            start = pl.multiple_of(t * tile_rows, tile_rows)
            return pltpu.make_async_copy(zeros, xs_ref.at[pl.ds(start, tile_rows), :], fill_sem)

        def fill_start(t, carry):
            @pl.when(fill_ref[t] == 1)
            def _():
                fill_copy(t).start()
            return carry

        def fill_wait(t, carry):
            @pl.when(fill_ref[t] == 1)
            def _():
                fill_copy(t).wait()
            return carry

        lax.fori_loop(0, n_tiles, fill_start, 0)
        lax.fori_loop(0, n_tiles, fill_wait, 0)

    def row_copy(r, k):
        src = pl.multiple_of(r * ROW_TILES, ROW_TILES)
        dst = pl.multiple_of(pos_ref[base + 2 * r + k], ROW_TILES)
        return pltpu.make_async_copy(u_ref.at[pl.ds(src, ROW_TILES), :],
                                     xs_ref.at[pl.ds(dst, ROW_TILES), :], sem)

    def issue(r, carry):
        row_copy(r, 0).start()
        row_copy(r, 1).start()
        return carry

    def drain(r, carry):
        row_copy(r, 0).wait()
        row_copy(r, 1).wait()
        return carry

    lax.fori_loop(0, DISPATCH_TM, issue, 0, unroll=DMA_UNROLL)
    lax.fori_loop(0, DISPATCH_TM, drain, 0, unroll=DMA_UNROLL)


def _dispatch(pos, tile_fill, u2, n_pad):
    n = u2.shape[0] // ROW_TILES
    return pl.pallas_call(
        _dispatch_kernel,
        out_shape=jax.ShapeDtypeStruct((n_pad * ROW_TILES, LANES), F32),
        grid_spec=pltpu.PrefetchScalarGridSpec(
            num_scalar_prefetch=2,
            grid=(n // DISPATCH_TM,),
            in_specs=[pl.BlockSpec((DISPATCH_TM * ROW_TILES, LANES), lambda i, p, f: (i, 0))],
            out_specs=pl.BlockSpec(memory_space=pl.ANY),
            scratch_shapes=[pltpu.VMEM((MOE_TILE * ROW_TILES, LANES), F32),
                            pltpu.SemaphoreType.DMA(()),
                            pltpu.SemaphoreType.DMA(())]),
        compiler_params=_params("arbitrary"),
        name="dispatch",
    )(pos, tile_fill, u2)


def _moe_kernel(te_ref, tr_ref, tv_ref, x_ref, wgu_ref, wd_ref, y_ref, wgu_bf, wd_bf):
    i = pl.program_id(0)
    fresh = jnp.logical_or(i == 0, te_ref[i] != te_ref[jnp.maximum(i - 1, 0)])

    @pl.when(jnp.logical_and(fresh, tv_ref[i] == 1))
    def _():
        wgu_bf[...] = wgu_ref[0].astype(BF16)
        wd_bf[...] = wd_ref[0].astype(BF16)

    @pl.when(tv_ref[i] == 1)
    def _():
        x = _load_token_rows(lambda g: x_ref[pl.ds(g, MOE_TILE, stride=ROW_TILES), :]).astype(BF16)
        hid = []
        for fb in range(EXPERT_FF // MOE_FF_BLOCK):
            gate = _dot(x, wgu_bf[:, fb * MOE_FF_BLOCK:(fb + 1) * MOE_FF_BLOCK])
            up = _dot(x, wgu_bf[:, EXPERT_FF + fb * MOE_FF_BLOCK:EXPERT_FF + (fb + 1) * MOE_FF_BLOCK])
            hid.append((_silu(gate) * up).astype(BF16))
        hid = jnp.concatenate(hid, axis=1)
        groups = MOE_OUT_BLOCK // LANES
        for cb in range(D // MOE_OUT_BLOCK):
            y = _dot(hid, wd_bf[:, cb * MOE_OUT_BLOCK:(cb + 1) * MOE_OUT_BLOCK])
            for g in range(groups):
                y_ref[pl.ds(cb * groups + g, MOE_TILE, stride=ROW_TILES), :] = y[:, g * LANES:(g + 1) * LANES]

    @pl.when(tv_ref[i] == 0)
    def _():
        y_ref[...] = jnp.zeros_like(y_ref)


def _moe(tile_expert, tile_row, tile_valid, xs, w_gate_up, w_down):
    tile_rows = MOE_TILE * ROW_TILES
    n_tiles = xs.shape[0] // tile_rows
    return pl.pallas_call(
        _moe_kernel,
        out_shape=jax.ShapeDtypeStruct(xs.shape, F32),
        grid_spec=pltpu.PrefetchScalarGridSpec(
            num_scalar_prefetch=3,
            grid=(n_tiles,),
            in_specs=[pl.BlockSpec((tile_rows, LANES), lambda i, te, tr, tv: (tr[i], 0)),
                      pl.BlockSpec((1, D, 2 * EXPERT_FF), lambda i, te, tr, tv: (te[i], 0, 0)),
                      pl.BlockSpec((1, EXPERT_FF, D), lambda i, te, tr, tv: (te[i], 0, 0))],
            out_specs=pl.BlockSpec((tile_rows, LANES), lambda i, te, tr, tv: (i, 0)),
            scratch_shapes=[pltpu.VMEM((D, 2 * EXPERT_FF), BF16),
                            pltpu.VMEM((EXPERT_FF, D), BF16)]),
        compiler_params=_params("arbitrary"),
        name="moe",
    )(tile_expert, tile_row, tile_valid, xs, w_gate_up, w_down)


COMBINE_TM = 256


def _combine_kernel(pos_ref, ys_ref, rw_ref, x1_ref, mod_ref, g2_ref, b2_ref, o_ref, buf, sems):
    i = pl.program_id(0)
    n_steps = pl.num_programs(0)

    def row_copy(step, slot, r, k):
        src = pl.multiple_of(pos_ref[step * (2 * COMBINE_TM) + 2 * r + k], ROW_TILES)
        dst = pl.multiple_of(r * ROW_TILES, ROW_TILES)
        return pltpu.make_async_copy(ys_ref.at[pl.ds(src, ROW_TILES), :],
                                     buf.at[slot, k, pl.ds(dst, ROW_TILES), :], sems.at[slot])

    def issue(step, slot):
        def body(r, carry):
            row_copy(step, slot, r, 0).start()
            row_copy(step, slot, r, 1).start()
            return carry
        lax.fori_loop(0, COMBINE_TM, body, 0, unroll=DMA_UNROLL)

    def drain(step, slot):
        def body(r, carry):
            row_copy(step, slot, r, 0).wait()
            row_copy(step, slot, r, 1).wait()
            return carry
        lax.fori_loop(0, COMBINE_TM, body, 0, unroll=DMA_UNROLL)

    slot = i % 2

    @pl.when(i == 0)
    def _():
        issue(0, 0)

    @pl.when(i + 1 < n_steps)
    def _():
        issue(i + 1, 1 - slot)

    drain(i, slot)

    rw = rw_ref[...]
    y0 = _load_token_rows(lambda s: buf[slot, 0, pl.ds(s, COMBINE_TM, stride=ROW_TILES), :])
    y1 = _load_token_rows(lambda s: buf[slot, 1, pl.ds(s, COMBINE_TM, stride=ROW_TILES), :])
    ffn = rw[:, 0:1] * y0 + rw[:, 1:2] * y1
    mod = mod_ref[0]
    o_ref[...] = _layer_norm(ALPHA * x1_ref[...] + mod[5:6] * ffn) * g2_ref[...] + b2_ref[...]


def _combine(pos, ys, rw, x1, mod3, ln_g, ln_b, seq):
    n = x1.shape[0]
    tm = COMBINE_TM
    tiles_per_batch = seq // tm
    return pl.pallas_call(
        _combine_kernel,
        out_shape=jax.ShapeDtypeStruct((n, D), F32),
        grid_spec=pltpu.PrefetchScalarGridSpec(
            num_scalar_prefetch=1,
            grid=(n // tm,),
            in_specs=[pl.BlockSpec(memory_space=pl.ANY),
                      pl.BlockSpec((tm, LANES), lambda i, p: (i, 0)),
                      pl.BlockSpec((tm, D), lambda i, p: (i, 0)),
                      pl.BlockSpec((1, 6, D), lambda i, p: (i // tiles_per_batch, 0, 0)),
                      pl.BlockSpec((1, D), lambda i, p: (0, 0)),
                      pl.BlockSpec((1, D), lambda i, p: (0, 0))],
            out_specs=pl.BlockSpec((tm, D), lambda i, p: (i, 0)),
            scratch_shapes=[pltpu.VMEM((2, 2, tm * ROW_TILES, LANES), F32),
                            pltpu.SemaphoreType.DMA((2,))]),
        compiler_params=_params("arbitrary"),
        name="combine",
    )(pos, ys, rw, x1, mod3, ln_g, ln_b)


def _in_weights(w_in):
    sizes = (SWA_Q_W, SWA_KV_W, SWA_KV_W, GDN_W, GDN_W, GDN_W, GDN_W, GDN_HEADS, GDN_HEADS, D, D)
    offs = [0]
    for s in sizes:
        offs.append(offs[-1] + s)
    w_t = w_in.T
    qa, ka, va, qb, kb, vb, zb, bl, al, ga, gb = (w_t[offs[k]:offs[k + 1]] for k in range(len(sizes)))

    def dup(w):
        w4 = w.reshape(KV_HEADS, 1, HEAD_DIM, D)
        return jnp.broadcast_to(w4, (KV_HEADS, 2, HEAD_DIM, D)).reshape(2 * SWA_KV_W, D)

    w_main_t = jnp.concatenate([qa, dup(ka), dup(va), qb, kb, vb, zb, 0.5 * ga, 0.5 * gb], axis=0).astype(BF16)
    small_t = jnp.concatenate([bl, al], axis=0)
    w_small = jnp.pad(small_t.T, ((0, 0), (0, LANES - 2 * GDN_HEADS))).astype(BF16)
    return w_main_t, w_small, small_t.astype(BF16)


def _route_plan(ri, cnt, n_tiles):
    counts = cnt[0, :N_EXPERTS].astype(I32)
    tiles = (counts + MOE_TILE - 1) // MOE_TILE
    tile_end = jnp.cumsum(tiles)
    row_off = (tile_end - tiles) * MOE_TILE
    expert = ri[:, 0:2]
    hit = expert[:, :, None] == jnp.arange(N_EXPERTS, dtype=I32)[None, None, :]
    pos = ((jnp.sum(jnp.where(hit, row_off[None, None, :], 0), axis=-1) + ri[:, 2:4]) * ROW_TILES).reshape(-1)
    used = tile_end[-1]
    t = jnp.arange(n_tiles, dtype=I32)
    t_eff = jnp.minimum(t, used - 1)
    tile_expert = jnp.minimum(jnp.sum(t_eff[:, None] >= tile_end[None, :], axis=1), N_EXPERTS - 1).astype(I32)
    tile_valid = (t < used).astype(I32)
    last_of_expert = jnp.any((t[:, None] + 1 == tile_end[None, :]) & (tiles[None, :] > 0), axis=1)
    tile_fill = jnp.logical_or(last_of_expert, t >= used).astype(I32)
    return pos.astype(I32), tile_expert, t_eff.astype(I32), tile_valid, tile_fill


def kernel(x, c, w_ada, b_ada, w_in, conv_w, swa_sinks, gdn_a_log, gdn_dt_bias, gdn_norm_w, w_proj_a, w_proj_b, w_out, ln1_g, ln1_b, w_router_group, b_router_group, w_router_expert, b_router_expert, w_gate_up, w_down, ln2_g, ln2_b):
    bsz, seq, _ = x.shape
    n = bsz * seq
    nc = seq // CHUNK
    for l in range(DEPTH):
        x2 = x.reshape(n, D)
        mod3 = _ada(c, w_ada[l], b_ada[l]).reshape(bsz, 6, D)
        w_main, w_small, w_small_t = _in_weights(w_in[l])
        hcat, sm, smt = _inproj(x2, mod3, w_main, w_small, w_small_t, seq)
        smt3 = smt.reshape(2 * GDN_HEADS, bsz * nc, CHUNK).transpose(1, 0, 2)

        oa = _swa(hcat, swa_sinks[l], bsz, seq)

        pad_lo = jnp.zeros((GDN_HEADS,), F32)
        prow = jnp.stack([jnp.pad(jnp.concatenate([pad_lo, gdn_a_log[l]]), (0, LANES - 2 * GDN_HEADS)),
                          jnp.pad(jnp.concatenate([pad_lo, gdn_dt_bias[l]]), (0, LANES - 2 * GDN_HEADS))])
        pcol = jnp.pad(jnp.stack([gdn_a_log[l], gdn_dt_bias[l]], axis=1), ((0, 0), (0, LANES - 2)))
        ob = _gdn(hcat, sm, smt3, conv_w[l], prow, pcol, gdn_norm_w[l].reshape(1, GDN_DIM), bsz, seq)

        def route_lanes(group_part, expert_part):
            rows = group_part.shape[0]
            gap = jnp.zeros((rows, ROUTE_EXPERT_LANE - N_GROUPS), F32)
            tail = jnp.zeros((rows, LANES - ROUTE_EXPERT_LANE - N_EXPERTS), F32)
            return jnp.concatenate([group_part, gap, expert_part, tail], axis=1)

        w_route = jnp.concatenate(_split_bf16(route_lanes(w_router_group[l], w_router_expert[l])), axis=1)
        b_route = route_lanes(b_router_group[l].reshape(1, N_GROUPS), b_router_expert[l].reshape(1, N_EXPERTS))
        x1, u2, ri, rw, cnt = _outproj(
            oa, ob, hcat, x2, mod3, w_proj_a[l].astype(BF16), w_proj_b[l].astype(BF16),
            w_out[l].astype(BF16), ln1_g[l].reshape(1, D), ln1_b[l].reshape(1, D), w_route, b_route, seq)

        n_tiles = (2 * n) // MOE_TILE + N_EXPERTS
        pos, tile_expert, tile_row, tile_valid, tile_fill = _route_plan(ri, cnt, n_tiles)
        xs = _dispatch(pos, tile_fill, u2, n_tiles * MOE_TILE)
        ys = _moe(tile_expert, tile_row, tile_valid, xs, w_gate_up[l], w_down[l])
        x2 = _combine(pos, ys, rw, x1, mod3, ln2_g[l].reshape(1, D), ln2_b[l].reshape(1, D), seq)
        x = x2.reshape(bsz, seq, D)
    return x
```

```python
import functools

import jax
import jax.numpy as jnp
from jax import lax
from jax.experimental import pallas as pl
from jax.experimental.pallas import tpu as pltpu

F32 = jnp.float32
BF16 = jnp.bfloat16
I32 = jnp.int32

D = 2048
CHUNK = 64
Q_HEADS = 16
KV_HEADS = 4
HEAD_DIM = 64
SWA_Q_W = 1024
SWA_KV_W = 256
GDN_HEADS = 8
GDN_DIM = 128
GDN_W = 1024
CONV_WIDTH = 4
N_GROUPS = 4
EPG = 8
N_EXPERTS = 32
EXPERT_FF = 512
DEPTH = 1
ALPHA = (2 * DEPTH) ** 0.25
LN_EPS = 1e-5
RMS_EPS = 1e-6
NEG_INF = -1e30

SPLIT_FACTOR = 65537.0
LANES = 128
VMEM_LIMIT = 56 * 1024 * 1024

COL_QA = 0
COL_K2 = 1024
COL_V2 = 1536
COL_QB = 2048
COL_KB = 3072
COL_VB = 4096
COL_ZB = 5120
COL_GA = 6144
COL_GB = 8192
H_WIDTH = 10240

ROUTE_EXPERT_LANE = 32
MOE_TILE = 256
MOE_FF_BLOCK = 256
MOE_OUT_BLOCK = 512
OUTPROJ_CB = 512
ROW_TILES = D // LANES


def _store_token_rows(ref, val):
    for g in range(ROW_TILES):
        ref[pl.ds(g, val.shape[0], stride=ROW_TILES), :] = val[:, g * LANES:(g + 1) * LANES]


def _load_token_rows(row_group):
    return jnp.concatenate([row_group(g) for g in range(ROW_TILES)], axis=1)


def _dot(a, b):
    return jnp.dot(a, b, preferred_element_type=F32)


def _dot_nt(a, b):
    return lax.dot_general(a, b, (((1,), (1,)), ((), ())), preferred_element_type=F32)


def _dot_tn(a, b):
    return lax.dot_general(a, b, (((0,), (0,)), ((), ())), preferred_element_type=F32)


def _split_bf16(v):
    c = v * SPLIT_FACTOR
    hi = c - (c - v)
    return hi.astype(BF16), (v - hi).astype(BF16)


def _silu(v):
    return v * jax.nn.sigmoid(v)


def _softplus(v):
    return jnp.maximum(v, 0.0) + jnp.log(1.0 + jnp.exp(-jnp.abs(v)))


def _layer_norm(v):
    mu = jnp.mean(v, axis=-1, keepdims=True)
    vc = v - mu
    var = jnp.mean(vc * vc, axis=-1, keepdims=True)
    return vc * lax.rsqrt(var + LN_EPS)


def _params(*sem):
    return pltpu.CompilerParams(dimension_semantics=sem, vmem_limit_bytes=VMEM_LIMIT)


def _ada_kernel(c_ref, w_ref, b_ref, o_ref):
    s_hi, s_lo = _split_bf16(_silu(c_ref[...]))
    w_hi, w_lo = _split_bf16(w_ref[...])
    o_ref[...] = _dot(s_hi, w_hi) + _dot(s_lo, w_hi) + _dot(s_hi, w_lo) + b_ref[...]


def _ada(c, w_ada, b_ada):
    bsz = c.shape[0]
    width = w_ada.shape[1]
    tn = 1024
    return pl.pallas_call(
        _ada_kernel,
        out_shape=jax.ShapeDtypeStruct((bsz, width), F32),
        grid=(width // tn,),
        in_specs=[pl.BlockSpec((bsz, D), lambda j: (0, 0)),
                  pl.BlockSpec((D, tn), lambda j: (0, j)),
                  pl.BlockSpec((1, tn), lambda j: (0, j))],
        out_specs=pl.BlockSpec((bsz, tn), lambda j: (0, j)),
        compiler_params=_params("arbitrary"),
        name="ada",
    )(c, w_ada, b_ada.reshape(1, width))


def _inproj_kernel(x_ref, mod_ref, w_ref, ws_ref, wst_ref, h_ref, sm_ref, smt_ref, u_scr):
    @pl.when(pl.program_id(1) == 0)
    def _():
        mod = mod_ref[0]
        u = _layer_norm(x_ref[...]) * (1.0 + mod[1:2]) + mod[0:1]
        ub = u.astype(BF16)
        u_scr[...] = ub
        sm_ref[...] = _dot(ub, ws_ref[...])
        smt_ref[...] = _dot_nt(wst_ref[...], ub)

    h_ref[...] = _dot_nt(u_scr[...], w_ref[...]).astype(BF16)


def _inproj(x2, mod3, w_main, w_small, w_small_t, seq):
    n = x2.shape[0]
    tm = min(1024, seq)
    tn = 1024
    tiles_per_batch = seq // tm
    return pl.pallas_call(
        _inproj_kernel,
        out_shape=(jax.ShapeDtypeStruct((n, H_WIDTH), BF16),
                   jax.ShapeDtypeStruct((n, LANES), F32),
                   jax.ShapeDtypeStruct((2 * GDN_HEADS, n), F32)),
        grid=(n // tm, H_WIDTH // tn),
        in_specs=[pl.BlockSpec((tm, D), lambda i, j: (i, 0)),
                  pl.BlockSpec((1, 6, D), lambda i, j: (i // tiles_per_batch, 0, 0)),
                  pl.BlockSpec((tn, D), lambda i, j: (j, 0)),
                  pl.BlockSpec((D, LANES), lambda i, j: (0, 0)),
                  pl.BlockSpec((2 * GDN_HEADS, D), lambda i, j: (0, 0))],
        out_specs=(pl.BlockSpec((tm, tn), lambda i, j: (i, j)),
                   pl.BlockSpec((tm, LANES), lambda i, j: (i, 0)),
                   pl.BlockSpec((2 * GDN_HEADS, tm), lambda i, j: (0, i))),
        scratch_shapes=[pltpu.VMEM((tm, D), BF16)],
        compiler_params=_params("arbitrary", "arbitrary"),
        name="inproj",
    )(x2, mod3, w_main, w_small, w_small_t)


SWA_TQ = 512
SWA_PREV = 128
SWA_BAND = 192


def _swa_kernel(sink_ref, q_ref, kp_ref, vp_ref, kc_ref, vc_ref, o_ref):
    i = pl.program_id(1)
    kwin = jnp.concatenate([kp_ref[...], kc_ref[...]], axis=0)
    vwin = jnp.concatenate([vp_ref[...], vc_ref[...]], axis=0)
    lo_lane = lax.broadcasted_iota(I32, (1, LANES), 1) < HEAD_DIM
    row_top = lax.broadcasted_iota(I32, (2 * CHUNK, 1), 0) < CHUNK
    key_iota = lax.broadcasted_iota(I32, (1, SWA_BAND), 1)
    zero = jnp.zeros((), BF16)
    for kv in range(KV_HEADS):
        k2 = kwin[:, kv * LANES:(kv + 1) * LANES]
        v2 = vwin[:, kv * LANES:(kv + 1) * LANES]
        k_lo = jnp.where(lo_lane, k2, zero)
        k_hi = jnp.where(lo_lane, zero, k2)
        v_lo = jnp.where(lo_lane, v2, zero)
        v_hi = jnp.where(lo_lane, zero, v2)
        sinks = (jnp.where(row_top, sink_ref[kv * 4 + 0], sink_ref[kv * 4 + 2]),
                 jnp.where(row_top, sink_ref[kv * 4 + 1], sink_ref[kv * 4 + 3]))
        base = kv * 2 * LANES
        chunks = range(SWA_TQ // CHUNK)
        items = [(c, par) for c in chunks for par in range(2)]
        ql = [jnp.concatenate([q_ref[c * CHUNK:(c + 1) * CHUNK, base:base + LANES],
                               q_ref[c * CHUNK:(c + 1) * CHUNK, base + LANES:base + 2 * LANES]], axis=0)
              for c in chunks]
        scores = [_dot_nt(ql[c], (k_lo, k_hi)[par][c * CHUNK:c * CHUNK + SWA_BAND]) * (HEAD_DIM ** -0.5)
                  for c, par in items]
        scores = [jnp.where((i * SWA_TQ - SWA_PREV + c * CHUNK + key_iota) >= 0, s, NEG_INF)
                  for (c, par), s in zip(items, scores)]
        top = [jnp.maximum(jnp.max(s, axis=-1, keepdims=True), sinks[par]) for (c, par), s in zip(items, scores)]
        probs = [jnp.exp(s - m) for s, m in zip(scores, top)]
        den = [jnp.sum(p, axis=-1, keepdims=True) + jnp.exp(sinks[par] - m)
               for (c, par), p, m in zip(items, probs, top)]
        outs = [_dot(p.astype(BF16), (v_lo, v_hi)[par][c * CHUNK:c * CHUNK + SWA_BAND]) / d
                for (c, par), p, d in zip(items, probs, den)]
        for c in chunks:
            rows = slice(c * CHUNK, (c + 1) * CHUNK)
            acc = outs[2 * c] + outs[2 * c + 1]
            o_ref[rows, base:base + LANES] = acc[0:CHUNK].astype(BF16)
            o_ref[rows, base + LANES:base + 2 * LANES] = acc[CHUNK:2 * CHUNK].astype(BF16)


def _swa(hcat, sinks, bsz, seq):
    n = hcat.shape[0]
    tq = SWA_TQ
    nq = seq // tq
    kvw = 2 * SWA_KV_W

    def prev_map(b, i, s):
        return (b * (seq // SWA_PREV) + jnp.maximum(i * (tq // SWA_PREV) - 1, 0), COL_K2 // kvw)

    def prev_map_v(b, i, s):
        return (b * (seq // SWA_PREV) + jnp.maximum(i * (tq // SWA_PREV) - 1, 0), COL_V2 // kvw)

    return pl.pallas_call(
        _swa_kernel,
        out_shape=jax.ShapeDtypeStruct((n, SWA_Q_W), BF16),
        grid_spec=pltpu.PrefetchScalarGridSpec(
            num_scalar_prefetch=1,
            grid=(bsz, nq),
            in_specs=[pl.BlockSpec((tq, SWA_Q_W), lambda b, i, s: (b * nq + i, COL_QA // SWA_Q_W)),
                      pl.BlockSpec((SWA_PREV, kvw), prev_map),
                      pl.BlockSpec((SWA_PREV, kvw), prev_map_v),
                      pl.BlockSpec((tq, kvw), lambda b, i, s: (b * nq + i, COL_K2 // kvw)),
                      pl.BlockSpec((tq, kvw), lambda b, i, s: (b * nq + i, COL_V2 // kvw))],
            out_specs=pl.BlockSpec((tq, SWA_Q_W), lambda b, i, s: (b * nq + i, 0))),
        compiler_params=_params("arbitrary", "arbitrary"),
        name="swa",
    )(sinks, hcat, hcat, hcat, hcat, hcat)


GDN_PREV = 16
GDN_CPB = 8
GDN_ROWS = GDN_CPB * CHUNK


def _mm(p, q):
    return _dot(p.astype(BF16), q.astype(BF16))


def _gdn_kernel(qc_ref, kc_ref, vc_ref, z_ref, qp_ref, kp_ref, vp_ref, cw_ref, sm_ref, smt_ref,
                prow_ref, pcol_ref, nw_ref, o_ref, state, xbuf):
    n = pl.program_id(1)
    heads = range(GDN_HEADS)

    @pl.when(n == 0)
    def _():
        state[...] = jnp.zeros_like(state)

    keep = jnp.where(n == 0, 0.0, 1.0)
    for s, (cur, prev) in enumerate(((qc_ref, qp_ref), (kc_ref, kp_ref), (vc_ref, vp_ref))):
        cols = slice(s * GDN_W, (s + 1) * GDN_W)
        xbuf[0:GDN_PREV, cols] = prev[...].astype(F32) * keep
        xbuf[GDN_PREV:GDN_PREV + GDN_ROWS, cols] = cur[...].astype(F32)
    conv = None
    for tap in range(CONV_WIDTH):
        start = GDN_PREV - (CONV_WIDTH - 1) + tap
        term = xbuf[start:start + GDN_ROWS, :] * cw_ref[tap:tap + 1, :]
        conv = term if conv is None else conv + term
    qkv = _silu(conv)

    ri = lax.broadcasted_iota(I32, (CHUNK, CHUNK), 0)
    ci = lax.broadcasted_iota(I32, (CHUNK, CHUNK), 1)
    incl = ri >= ci
    strict = ri > ci
    eye = jnp.where(ri == ci, 1.0, 0.0).astype(F32)
    blk8 = (ri // 8) == (ci // 8)
    blk16 = (ri // 16) == (ci // 16)
    blk32 = (ri // 32) == (ci // 32)
    levels = (blk16 & ~blk8, blk32 & ~blk16, ~blk32)
    tri_lo = jnp.where(incl, 1.0, 0.0).astype(BF16)
    tri_up = jnp.where(ri <= ci, 1.0, 0.0).astype(BF16)

    staged = []

    def advance(c, s_old):
        rows = slice(c * CHUNK, (c + 1) * CHUNK)
        u_c, wq, qk, k_dec, cdec = staged[c]
        ws = [_dot(wq[h], s_old[h].astype(BF16)) for h in heads]
        vb = [(u_c[h] - ws[h][0:CHUNK]).astype(BF16) for h in heads]
        o_c = [ws[h][CHUNK:2 * CHUNK] + _dot(qk[h], vb[h]) for h in heads]
        s_new = [cdec[h] * s_old[h] + _dot_tn(k_dec[h], vb[h]) for h in heads]
        for h in heads:
            lanes = slice(h * GDN_DIM, (h + 1) * GDN_DIM)
            o_n = o_c[h] * lax.rsqrt(jnp.mean(o_c[h] * o_c[h], axis=-1, keepdims=True) + RMS_EPS)
            o_n = o_n * nw_ref[...] * _silu(z_ref[rows, lanes].astype(F32))
            o_ref[rows, lanes] = o_n.astype(BF16)
        return s_new

    s_cur = [state[h] for h in heads]
    for c in range(GDN_CPB):
        rows = slice(c * CHUNK, (c + 1) * CHUNK)
        sm = sm_ref[rows, :]
        beta_all = jax.nn.sigmoid(sm)
        g_all = -jnp.exp(prow_ref[0:1, :]) * _softplus(sm + prow_ref[1:2, :])
        g_hi, g_lo = _split_bf16(g_all)
        cum_all = _dot(tri_lo, g_hi) + _dot(tri_lo, g_lo)
        ecum_all = jnp.exp(cum_all)
        cum_last = cum_all[CHUNK - 1:CHUNK, :]
        kscale_all = jnp.exp(cum_last - cum_all)
        cdec_all = jnp.exp(cum_last)
        smt = smt_ref[c]
        g_row = -jnp.exp(pcol_ref[:, 0:1]) * _softplus(smt[GDN_HEADS:2 * GDN_HEADS, :] + pcol_ref[:, 1:2])
        gr_hi, gr_lo = _split_bf16(g_row)
        cum_row = _dot(gr_hi, tri_up) + _dot(gr_lo, tri_up)

        def col(arr, lane):
            return arr[:, lane:lane + 1]

        qs, ks, vs = [], [], []
        for h in heads:
            qh = qkv[rows, h * GDN_DIM:(h + 1) * GDN_DIM]
            kh = qkv[rows, GDN_W + h * GDN_DIM:GDN_W + (h + 1) * GDN_DIM]
            qs.append(qh * lax.rsqrt(jnp.sum(qh * qh, axis=-1, keepdims=True) + RMS_EPS) * (GDN_DIM ** -0.5))
            ks.append(kh * lax.rsqrt(jnp.sum(kh * kh, axis=-1, keepdims=True) + RMS_EPS))
            vs.append(qkv[rows, 2 * GDN_W + h * GDN_DIM:2 * GDN_W + (h + 1) * GDN_DIM])
        beta = [col(beta_all, h) for h in heads]
        ecum = [col(ecum_all, GDN_HEADS + h) for h in heads]
        dec = []
        for h in heads:
            diff = col(cum_all, GDN_HEADS + h) - cum_row[h:h + 1, :]
            dec.append(jnp.where(incl, jnp.exp(jnp.where(incl, diff, 0.0)), 0.0))
        kb = [ks[h].astype(BF16) for h in heads]
        qkk = [_dot_nt(jnp.concatenate([qs[h].astype(BF16), kb[h]], axis=0), kb[h]) for h in heads]
        qk = [(qkk[h][0:CHUNK] * dec[h]).astype(BF16) for h in heads]
        a_mat = [jnp.where(strict, beta[h] * qkk[h][CHUNK:2 * CHUNK] * dec[h], 0.0) for h in heads]

        a8 = [jnp.where(blk8, a_mat[h], 0.0) for h in heads]
        a8_2 = [_mm(a8[h], a8[h]) for h in heads]
        a8_4 = [_mm(a8_2[h], a8_2[h]) for h in heads]
        t = [_mm(eye - a8[h], eye + a8_2[h]) for h in heads]
        t = [_mm(t[h], eye + a8_4[h]) for h in heads]
        for level in levels:
            inner = [_mm(jnp.where(level, a_mat[h], 0.0), t[h]) for h in heads]
            t = [t[h] - _mm(t[h], inner[h]) for h in heads]

        uw = [_mm(t[h], jnp.concatenate([vs[h] * beta[h], ks[h] * (beta[h] * ecum[h])], axis=1))
              for h in heads]
        wq = [jnp.concatenate([uw[h][:, GDN_DIM:2 * GDN_DIM], qs[h] * ecum[h]], axis=0).astype(BF16)
              for h in heads]
        k_dec = [(ks[h] * col(kscale_all, GDN_HEADS + h)).astype(BF16) for h in heads]
        cdec = [col(cdec_all, GDN_HEADS + h) for h in heads]
        staged.append(([uw[h][:, 0:GDN_DIM] for h in heads], wq, qk, k_dec, cdec))

    for c in range(GDN_CPB):
        s_cur = advance(c, s_cur)
    for h in heads:
        state[h] = s_cur[h]


def _gdn(hcat, sm, smt3, conv_w, prow, pcol, norm_w, bsz, seq):
    n = hcat.shape[0]
    steps = seq // GDN_ROWS
    pb = GDN_ROWS // GDN_PREV

    def cur(col):
        return pl.BlockSpec((GDN_ROWS, GDN_W), lambda b, c: (b * steps + c, col // GDN_W))

    def prev(col):
        return pl.BlockSpec((GDN_PREV, GDN_W),
                            lambda b, c: (b * steps * pb + jnp.maximum(c * pb - 1, 0), col // GDN_W))

    return pl.pallas_call(
        _gdn_kernel,
        out_shape=jax.ShapeDtypeStruct((n, GDN_W), BF16),
        grid=(bsz, steps),
        in_specs=[cur(COL_QB), cur(COL_KB), cur(COL_VB), cur(COL_ZB),
                  prev(COL_QB), prev(COL_KB), prev(COL_VB),
                  pl.BlockSpec((CONV_WIDTH, 3 * GDN_W), lambda b, c: (0, 0)),
                  pl.BlockSpec((GDN_ROWS, LANES), lambda b, c: (b * steps + c, 0)),
                  pl.BlockSpec((GDN_CPB, 2 * GDN_HEADS, CHUNK), lambda b, c: (b * steps + c, 0, 0)),
                  pl.BlockSpec((2, LANES), lambda b, c: (0, 0)),
                  pl.BlockSpec((GDN_HEADS, LANES), lambda b, c: (0, 0)),
                  pl.BlockSpec((1, GDN_DIM), lambda b, c: (0, 0))],
        out_specs=pl.BlockSpec((GDN_ROWS, GDN_W), lambda b, c: (b * steps + c, 0)),
        scratch_shapes=[pltpu.VMEM((GDN_HEADS, GDN_DIM, GDN_DIM), F32),
                        pltpu.VMEM((GDN_PREV + GDN_ROWS, 3 * GDN_W), F32)],
        compiler_params=_params("arbitrary", "arbitrary"),
        name="gdn",
    )(hcat, hcat, hcat, hcat, hcat, hcat, hcat, conv_w, sm, smt3, prow, pcol, norm_w)


OUTPROJ_TM = 256


def _outproj_step(i, oa_ref, ob_ref, ga_ref, gb_ref, x_ref, mod_ref, wa_ref, wb_ref, wo_ref,
                  g1_ref, b1_ref, wr_ref, br_ref, x1_ref, u2_ref, ri_ref, rw_ref, cnt_ref, run,
                  merged_scr, mix_w, mix_r):
    tm = OUTPROJ_TM
    mod = mod_ref[0]
    oa = oa_ref[...]
    ob = ob_ref[...]
    lane = lax.broadcasted_iota(I32, (1, LANES), 1)
    big = jnp.int32(LANES)

    def project(cb):
        cs = slice(cb * OUTPROJ_CB, (cb + 1) * OUTPROJ_CB)
        merged = ((jnp.tanh(ga_ref[:, cs].astype(F32)) + 1.0) * _dot(oa, wa_ref[:, cs])
                  + (jnp.tanh(gb_ref[:, cs].astype(F32)) + 1.0) * _dot(ob, wb_ref[:, cs]))
        merged_scr[:, cs] = merged.astype(BF16)

    def out_project(half):
        cs = slice(half * (D // 2), (half + 1) * (D // 2))
        mix_w[:, cs] = _dot(merged_scr[...], wo_ref[:, cs])

    project(0)
    project(1)

    x1 = _layer_norm(ALPHA * x_ref[...] + (0.5 * mod[2:3]) * mix_r[...]) * g1_ref[...] + b1_ref[...]
    x1_ref[...] = x1

    project(2)
    project(3)

    u2 = _layer_norm(x1) * (1.0 + mod[4:5]) + mod[3:4]
    _store_token_rows(u2_ref, u2)
    u_hi, u_lo = _split_bf16(u2)

    out_project(0)

    hi_pass = _dot(u_hi, wr_ref[...])
    logits = (hi_pass[:, 0:LANES] + hi_pass[:, LANES:2 * LANES]
              + _dot(u_lo, wr_ref[:, 0:LANES]) + br_ref[...])
    gmask = lane < N_GROUPS
    lg = jnp.where(gmask, logits, NEG_INF)
    gmax = jnp.max(lg, axis=-1, keepdims=True)
    gidx = jnp.min(jnp.where(lg == gmax, lane, big), axis=-1, keepdims=True)
    p_group = 1.0 / jnp.sum(jnp.exp(lg - gmax), axis=-1, keepdims=True)
    emask = (lane >> 3) == (gidx + ROUTE_EXPERT_LANE // EPG)
    le = jnp.where(emask, logits, NEG_INF)
    m1 = jnp.max(le, axis=-1, keepdims=True)
    i1 = jnp.min(jnp.where(le == m1, lane, big), axis=-1, keepdims=True)
    le2 = jnp.where(lane == i1, NEG_INF, le)
    m2 = jnp.max(le2, axis=-1, keepdims=True)
    i2 = jnp.min(jnp.where(le2 == m2, lane, big), axis=-1, keepdims=True)
    e2_rel = jnp.exp(m2 - m1)
    wgt1 = p_group / (1.0 + e2_rel)
    wgt2 = p_group * e2_rel / (1.0 + e2_rel)
    e1 = i1 - ROUTE_EXPERT_LANE
    e2 = i2 - ROUTE_EXPERT_LANE

    out_project(1)

    hot1 = lane == e1
    hot2 = lane == e2
    onehot = jnp.where(hot1 | hot2, 1.0, 0.0).astype(F32)
    tr = lax.broadcasted_iota(I32, (tm, tm), 0)
    tc = lax.broadcasted_iota(I32, (tm, tm), 1)
    before = jnp.where(tr > tc, 1.0, 0.0).astype(BF16)
    total = run[...] + _dot(before, onehot.astype(BF16))
    r1 = jnp.sum(jnp.where(hot1, total, 0.0), axis=-1, keepdims=True).astype(I32)
    r2 = jnp.sum(jnp.where(hot2, total, 0.0), axis=-1, keepdims=True).astype(I32)
    live = jnp.where(i == 0, 0.0, 1.0)
    run[...] = (run[...] + jnp.sum(onehot, axis=0, keepdims=True)) * live
    cnt_ref[...] = run[...]
    ri_ref[...] = jnp.where(lane == 0, e1, jnp.where(lane == 1, e2, jnp.where(lane == 2, r1, r2)))
    rw_ref[...] = jnp.where(lane == 0, wgt1, wgt2)


def _outproj_kernel(*refs):
    run, merged_scr, mix_a, mix_b = refs[-4:]
    i = pl.program_id(0)

    @pl.when(i == 0)
    def _():
        run[...] = jnp.zeros_like(run)
        mix_b[...] = jnp.zeros_like(mix_b)

    @pl.when(i % 2 == 0)
    def _():
        _outproj_step(i, *refs[:-2], mix_a, mix_b)

    @pl.when(i % 2 == 1)
    def _():
        _outproj_step(i, *refs[:-2], mix_b, mix_a)


def _outproj(oa, ob, hcat, x2, mod3, wa, wb, wo, ln_g, ln_b, w_route, b_route, seq):
    n = x2.shape[0]
    tm = OUTPROJ_TM
    n_tiles = n // tm
    tiles_per_batch = seq // tm
    const = dict(pipeline_mode=pl.Buffered(1))

    def ahead(i):
        return jnp.minimum(i, n_tiles - 1)

    def behind(i):
        return jnp.maximum(i - 1, 0)

    return pl.pallas_call(
        _outproj_kernel,
        out_shape=(jax.ShapeDtypeStruct((n, D), F32),
                   jax.ShapeDtypeStruct((n * ROW_TILES, LANES), F32),
                   jax.ShapeDtypeStruct((n, LANES), I32),
                   jax.ShapeDtypeStruct((n, LANES), F32),
                   jax.ShapeDtypeStruct((1, LANES), F32)),
        grid=(n_tiles + 1,),
        in_specs=[pl.BlockSpec((tm, SWA_Q_W), lambda i: (ahead(i), 0)),
                  pl.BlockSpec((tm, GDN_W), lambda i: (ahead(i), 0)),
                  pl.BlockSpec((tm, D), lambda i: (ahead(i), COL_GA // D)),
                  pl.BlockSpec((tm, D), lambda i: (ahead(i), COL_GB // D)),
                  pl.BlockSpec((tm, D), lambda i: (behind(i), 0)),
                  pl.BlockSpec((1, 6, D), lambda i: (behind(i) // tiles_per_batch, 0, 0)),
                  pl.BlockSpec((SWA_Q_W, D), lambda i: (0, 0), **const),
                  pl.BlockSpec((GDN_W, D), lambda i: (0, 0), **const),
                  pl.BlockSpec((D, D), lambda i: (0, 0), **const),
                  pl.BlockSpec((1, D), lambda i: (0, 0)),
                  pl.BlockSpec((1, D), lambda i: (0, 0)),
                  pl.BlockSpec((D, 2 * LANES), lambda i: (0, 0), **const),
                  pl.BlockSpec((1, LANES), lambda i: (0, 0))],
        out_specs=(pl.BlockSpec((tm, D), lambda i: (behind(i), 0)),
                   pl.BlockSpec((tm * ROW_TILES, LANES), lambda i: (behind(i), 0)),
                   pl.BlockSpec((tm, LANES), lambda i: (behind(i), 0)),
                   pl.BlockSpec((tm, LANES), lambda i: (behind(i), 0)),
                   pl.BlockSpec((1, LANES), lambda i: (0, 0))),
        scratch_shapes=[pltpu.VMEM((1, LANES), F32),
                        pltpu.VMEM((tm, D), BF16),
                        pltpu.VMEM((tm, D), F32),
                        pltpu.VMEM((tm, D), F32)],
        compiler_params=_params("arbitrary"),
        name="outproj",
    )(oa, ob, hcat, hcat, x2, mod3, wa, wb, wo, ln_g, ln_b, w_route, b_route)


DISPATCH_TM = 512


DMA_UNROLL = 8


def _dispatch_kernel(pos_ref, fill_ref, u_ref, xs_ref, zeros, sem, fill_sem):
    base = pl.program_id(0) * (2 * DISPATCH_TM)
    tile_rows = MOE_TILE * ROW_TILES
    n_tiles = xs_ref.shape[0] // tile_rows

    @pl.when(pl.program_id(0) == 0)
    def _():
        zeros[...] = jnp.zeros_like(zeros)

        def fill_copy(t):
            start = pl.multiple_of(t * tile_rows, tile_rows)
            return pltpu.make_async_copy(zeros, xs_ref.at[pl.ds(start, tile_rows), :], fill_sem)

        def fill_start(t, carry):
            @pl.when(fill_ref[t] == 1)
            def _():
                fill_copy(t).start()
            return carry

        def fill_wait(t, carry):
            @pl.when(fill_ref[t] == 1)
            def _():
                fill_copy(t).wait()
            return carry

        lax.fori_loop(0, n_tiles, fill_start, 0)
        lax.fori_loop(0, n_tiles, fill_wait, 0)

    def row_copy(r, k):
        src = pl.multiple_of(r * ROW_TILES, ROW_TILES)
        dst = pl.multiple_of(pos_ref[base + 2 * r + k], ROW_TILES)
        return pltpu.make_async_copy(u_ref.at[pl.ds(src, ROW_TILES), :],
                                     xs_ref.at[pl.ds(dst, ROW_TILES), :], sem)

    def issue(r, carry):
        row_copy(r, 0).start()
        row_copy(r, 1).start()
        return carry

    def drain(r, carry):
        row_copy(r, 0).wait()
        row_copy(r, 1).wait()
        return carry

    lax.fori_loop(0, DISPATCH_TM, issue, 0, unroll=DMA_UNROLL)
    lax.fori_loop(0, DISPATCH_TM, drain, 0, unroll=DMA_UNROLL)


def _dispatch(pos, tile_fill, u2, n_pad):
    n = u2.shape[0] // ROW_TILES
    return pl.pallas_call(
        _dispatch_kernel,
        out_shape=jax.ShapeDtypeStruct((n_pad * ROW_TILES, LANES), F32),
        grid_spec=pltpu.PrefetchScalarGridSpec(
            num_scalar_prefetch=2,
            grid=(n // DISPATCH_TM,),
            in_specs=[pl.BlockSpec((DISPATCH_TM * ROW_TILES, LANES), lambda i, p, f: (i, 0))],
            out_specs=pl.BlockSpec(memory_space=pl.ANY),
            scratch_shapes=[pltpu.VMEM((MOE_TILE * ROW_TILES, LANES), F32),
                            pltpu.SemaphoreType.DMA(()),
                            pltpu.SemaphoreType.DMA(())]),
        compiler_params=_params("arbitrary"),
        name="dispatch",
    )(pos, tile_fill, u2)


def _moe_kernel(te_ref, tr_ref, tv_ref, x_ref, wgu_ref, wd_ref, y_ref, wgu_bf, wd_bf):
    i = pl.program_id(0)
    fresh = jnp.logical_or(i == 0, te_ref[i] != te_ref[jnp.maximum(i - 1, 0)])

    @pl.when(jnp.logical_and(fresh, tv_ref[i] == 1))
    def _():
        wgu_bf[...] = wgu_ref[0].astype(BF16)
        wd_bf[...] = wd_ref[0].astype(BF16)

    @pl.when(tv_ref[i] == 1)
    def _():
        x = _load_token_rows(lambda g: x_ref[pl.ds(g, MOE_TILE, stride=ROW_TILES), :]).astype(BF16)
        hid = []
        for fb in range(EXPERT_FF // MOE_FF_BLOCK):
            gate = _dot(x, wgu_bf[:, fb * MOE_FF_BLOCK:(fb + 1) * MOE_FF_BLOCK])
            up = _dot(x, wgu_bf[:, EXPERT_FF + fb * MOE_FF_BLOCK:EXPERT_FF + (fb + 1) * MOE_FF_BLOCK])
            hid.append((_silu(gate) * up).astype(BF16))
        hid = jnp.concatenate(hid, axis=1)
        groups = MOE_OUT_BLOCK // LANES
        for cb in range(D // MOE_OUT_BLOCK):
            y = _dot(hid, wd_bf[:, cb * MOE_OUT_BLOCK:(cb + 1) * MOE_OUT_BLOCK])
            for g in range(groups):
                y_ref[pl.ds(cb * groups + g, MOE_TILE, stride=ROW_TILES), :] = y[:, g * LANES:(g + 1) * LANES]

    @pl.when(tv_ref[i] == 0)
    def _():
        y_ref[...] = jnp.zeros_like(y_ref)


def _moe(tile_expert, tile_row, tile_valid, xs, w_gate_up, w_down):
    tile_rows = MOE_TILE * ROW_TILES
    n_tiles = xs.shape[0] // tile_rows
    return pl.pallas_call(
        _moe_kernel,
        out_shape=jax.ShapeDtypeStruct(xs.shape, F32),
        grid_spec=pltpu.PrefetchScalarGridSpec(
            num_scalar_prefetch=3,
            grid=(n_tiles,),
            in_specs=[pl.BlockSpec((tile_rows, LANES), lambda i, te, tr, tv: (tr[i], 0)),
                      pl.BlockSpec((1, D, 2 * EXPERT_FF), lambda i, te, tr, tv: (te[i], 0, 0)),
                      pl.BlockSpec((1, EXPERT_FF, D), lambda i, te, tr, tv: (te[i], 0, 0))],
            out_specs=pl.BlockSpec((tile_rows, LANES), lambda i, te, tr, tv: (i, 0)),
            scratch_shapes=[pltpu.VMEM((D, 2 * EXPERT_FF), BF16),
                            pltpu.VMEM((EXPERT_FF, D), BF16)]),
        compiler_params=_params("arbitrary"),
        name="moe",
    )(tile_expert, tile_row, tile_valid, xs, w_gate_up, w_down)


COMBINE_TM = 256


def _combine_kernel(pos_ref, ys_ref, rw_ref, x1_ref, mod_ref, g2_ref, b2_ref, o_ref, buf, sems):
    i = pl.program_id(0)
    n_steps = pl.num_programs(0)

    def row_copy(step, slot, r, k):
        src = pl.multiple_of(pos_ref[step * (2 * COMBINE_TM) + 2 * r + k], ROW_TILES)
        dst = pl.multiple_of(r * ROW_TILES, ROW_TILES)
        return pltpu.make_async_copy(ys_ref.at[pl.ds(src, ROW_TILES), :],
                                     buf.at[slot, k, pl.ds(dst, ROW_TILES), :], sems.at[slot])

    def issue(step, slot):
        def body(r, carry):
            row_copy(step, slot, r, 0).start()
            row_copy(step, slot, r, 1).start()
            return carry
        lax.fori_loop(0, COMBINE_TM, body, 0, unroll=DMA_UNROLL)

    def drain(step, slot):
        def body(r, carry):
            row_copy(step, slot, r, 0).wait()
            row_copy(step, slot, r, 1).wait()
            return carry
        lax.fori_loop(0, COMBINE_TM, body, 0, unroll=DMA_UNROLL)

    slot = i % 2

    @pl.when(i == 0)
    def _():
        issue(0, 0)

    @pl.when(i + 1 < n_steps)
    def _():
        issue(i + 1, 1 - slot)

    drain(i, slot)

    rw = rw_ref[...]
    y0 = _load_token_rows(lambda s: buf[slot, 0, pl.ds(s, COMBINE_TM, stride=ROW_TILES), :])
    y1 = _load_token_rows(lambda s: buf[slot, 1, pl.ds(s, COMBINE_TM, stride=ROW_TILES), :])
    ffn = rw[:, 0:1] * y0 + rw[:, 1:2] * y1
    mod = mod_ref[0]
    o_ref[...] = _layer_norm(ALPHA * x1_ref[...] + mod[5:6] * ffn) * g2_ref[...] + b2_ref[...]


def _combine(pos, ys, rw, x1, mod3, ln_g, ln_b, seq):
    n = x1.shape[0]
    tm = COMBINE_TM
    tiles_per_batch = seq // tm
    return pl.pallas_call(
        _combine_kernel,
        out_shape=jax.ShapeDtypeStruct((n, D), F32),
        grid_spec=pltpu.PrefetchScalarGridSpec(
            num_scalar_prefetch=1,
            grid=(n // tm,),
            in_specs=[pl.BlockSpec(memory_space=pl.ANY),
                      pl.BlockSpec((tm, LANES), lambda i, p: (i, 0)),
                      pl.BlockSpec((tm, D), lambda i, p: (i, 0)),
                      pl.BlockSpec((1, 6, D), lambda i, p: (i // tiles_per_batch, 0, 0)),
                      pl.BlockSpec((1, D), lambda i, p: (0, 0)),
                      pl.BlockSpec((1, D), lambda i, p: (0, 0))],
            out_specs=pl.BlockSpec((tm, D), lambda i, p: (i, 0)),
            scratch_shapes=[pltpu.VMEM((2, 2, tm * ROW_TILES, LANES), F32),
                            pltpu.SemaphoreType.DMA((2,))]),
        compiler_params=_params("arbitrary"),
        name="combine",
    )(pos, ys, rw, x1, mod3, ln_g, ln_b)


def _in_weights(w_in):
    sizes = (SWA_Q_W, SWA_KV_W, SWA_KV_W, GDN_W, GDN_W, GDN_W, GDN_W, GDN_HEADS, GDN_HEADS, D, D)
    offs = [0]
    for s in sizes:
        offs.append(offs[-1] + s)
    w_t = w_in.T
    qa, ka, va, qb, kb, vb, zb, bl, al, ga, gb = (w_t[offs[k]:offs[k + 1]] for k in range(len(sizes)))

    def dup(w):
        w4 = w.reshape(KV_HEADS, 1, HEAD_DIM, D)
        return jnp.broadcast_to(w4, (KV_HEADS, 2, HEAD_DIM, D)).reshape(2 * SWA_KV_W, D)

    w_main_t = jnp.concatenate([qa, dup(ka), dup(va), qb, kb, vb, zb, 0.5 * ga, 0.5 * gb], axis=0).astype(BF16)
    small_t = jnp.concatenate([bl, al], axis=0)
    w_small = jnp.pad(small_t.T, ((0, 0), (0, LANES - 2 * GDN_HEADS))).astype(BF16)
    return w_main_t, w_small, small_t.astype(BF16)


def _route_plan(ri, cnt, n_tiles):
    counts = cnt[0, :N_EXPERTS].astype(I32)
    tiles = (counts + MOE_TILE - 1) // MOE_TILE
    tile_end = jnp.cumsum(tiles)
    row_off = (tile_end - tiles) * MOE_TILE
    expert = ri[:, 0:2]
    hit = expert[:, :, None] == jnp.arange(N_EXPERTS, dtype=I32)[None, None, :]
    pos = ((jnp.sum(jnp.where(hit, row_off[None, None, :], 0), axis=-1) + ri[:, 2:4]) * ROW_TILES).reshape(-1)
    used = tile_end[-1]
    t = jnp.arange(n_tiles, dtype=I32)
    t_eff = jnp.minimum(t, used - 1)
    tile_expert = jnp.minimum(jnp.sum(t_eff[:, None] >= tile_end[None, :], axis=1), N_EXPERTS - 1).astype(I32)
    tile_valid = (t < used).astype(I32)
    last_of_expert = jnp.any((t[:, None] + 1 == tile_end[None, :]) & (tiles[None, :] > 0), axis=1)
    tile_fill = jnp.logical_or(last_of_expert, t >= used).astype(I32)
    return pos.astype(I32), tile_expert, t_eff.astype(I32), tile_valid, tile_fill


def kernel(x, c, w_ada, b_ada, w_in, conv_w, swa_sinks, gdn_a_log, gdn_dt_bias, gdn_norm_w, w_proj_a, w_proj_b, w_out, ln1_g, ln1_b, w_router_group, b_router_group, w_router_expert, b_router_expert, w_gate_up, w_down, ln2_g, ln2_b):
    bsz, seq, _ = x.shape
    n = bsz * seq
    nc = seq // CHUNK
    for l in range(DEPTH):
        x2 = x.reshape(n, D)
        mod3 = _ada(c, w_ada[l], b_ada[l]).reshape(bsz, 6, D)
        w_main, w_small, w_small_t = _in_weights(w_in[l])
        hcat, sm, smt = _inproj(x2, mod3, w_main, w_small, w_small_t, seq)
        smt3 = smt.reshape(2 * GDN_HEADS, bsz * nc, CHUNK).transpose(1, 0, 2)

        oa = _swa(hcat, swa_sinks[l], bsz, seq)

        pad_lo = jnp.zeros((GDN_HEADS,), F32)
        prow = jnp.stack([jnp.pad(jnp.concatenate([pad_lo, gdn_a_log[l]]), (0, LANES - 2 * GDN_HEADS)),
                          jnp.pad(jnp.concatenate([pad_lo, gdn_dt_bias[l]]), (0, LANES - 2 * GDN_HEADS))])
        pcol = jnp.pad(jnp.stack([gdn_a_log[l], gdn_dt_bias[l]], axis=1), ((0, 0), (0, LANES - 2)))
        ob = _gdn(hcat, sm, smt3, conv_w[l], prow, pcol, gdn_norm_w[l].reshape(1, GDN_DIM), bsz, seq)

        def route_lanes(group_part, expert_part):
            rows = group_part.shape[0]
            gap = jnp.zeros((rows, ROUTE_EXPERT_LANE - N_GROUPS), F32)
            tail = jnp.zeros((rows, LANES - ROUTE_EXPERT_LANE - N_EXPERTS), F32)
            return jnp.concatenate([group_part, gap, expert_part, tail], axis=1)

        w_route = jnp.concatenate(_split_bf16(route_lanes(w_router_group[l], w_router_expert[l])), axis=1)
        b_route = route_lanes(b_router_group[l].reshape(1, N_GROUPS), b_router_expert[l].reshape(1, N_EXPERTS))
        x1, u2, ri, rw, cnt = _outproj(
            oa, ob, hcat, x2, mod3, w_proj_a[l].astype(BF16), w_proj_b[l].astype(BF16),
            w_out[l].astype(BF16), ln1_g[l].reshape(1, D), ln1_b[l].reshape(1, D), w_route, b_route, seq)

        n_tiles = (2 * n) // MOE_TILE + N_EXPERTS
        pos, tile_expert, tile_row, tile_valid, tile_fill = _route_plan(ri, cnt, n_tiles)
        xs = _dispatch(pos, tile_fill, u2, n_tiles * MOE_TILE)
        ys = _moe(tile_expert, tile_row, tile_valid, xs, w_gate_up[l], w_down[l])
        x2 = _combine(pos, ys, rw, x1, mod3, ln2_g[l].reshape(1, D), ln2_b[l].reshape(1, D), seq)
        x = x2.reshape(bsz, seq, D)
    return x
```

```python
import functools

import jax
import jax.numpy as jnp
from jax import lax
from jax.experimental import pallas as pl
from jax.experimental.pallas import tpu as pltpu

F32 = jnp.float32
BF16 = jnp.bfloat16
I32 = jnp.int32

D = 2048
CHUNK = 64
Q_HEADS = 16
KV_HEADS = 4
HEAD_DIM = 64
SWA_Q_W = 1024
SWA_KV_W = 256
GDN_HEADS = 8
GDN_DIM = 128
GDN_W = 1024
CONV_WIDTH = 4
N_GROUPS = 4
EPG = 8
N_EXPERTS = 32
EXPERT_FF = 512
DEPTH = 1
ALPHA = (2 * DEPTH) ** 0.25
LN_EPS = 1e-5
RMS_EPS = 1e-6
NEG_INF = -1e30

SPLIT_FACTOR = 65537.0
LANES = 128
VMEM_LIMIT = 56 * 1024 * 1024

COL_QA = 0
COL_K2 = 1024
COL_V2 = 1536
COL_QB = 2048
COL_KB = 3072
COL_VB = 4096
COL_ZB = 5120
COL_GA = 6144
COL_GB = 8192
H_WIDTH = 10240

ROUTE_EXPERT_LANE = 32
MOE_TILE = 256
MOE_FF_BLOCK = 256
MOE_OUT_BLOCK = 512
OUTPROJ_CB = 512
ROW_TILES = D // LANES


def _store_token_rows(ref, val):
    for g in range(ROW_TILES):
        ref[pl.ds(g, val.shape[0], stride=ROW_TILES), :] = val[:, g * LANES:(g + 1) * LANES]


def _load_token_rows(row_group):
    return jnp.concatenate([row_group(g) for g in range(ROW_TILES)], axis=1)


def _dot(a, b):
    return jnp.dot(a, b, preferred_element_type=F32)


def _dot_nt(a, b):
    return lax.dot_general(a, b, (((1,), (1,)), ((), ())), preferred_element_type=F32)


def _dot_tn(a, b):
    return lax.dot_general(a, b, (((0,), (0,)), ((), ())), preferred_element_type=F32)


def _split_bf16(v):
    c = v * SPLIT_FACTOR
    hi = c - (c - v)
    return hi.astype(BF16), (v - hi).astype(BF16)


def _silu(v):
    return v * jax.nn.sigmoid(v)


def _softplus(v):
    return jnp.maximum(v, 0.0) + jnp.log(1.0 + jnp.exp(-jnp.abs(v)))


def _layer_norm(v):
    mu = jnp.mean(v, axis=-1, keepdims=True)
    vc = v - mu
    var = jnp.mean(vc * vc, axis=-1, keepdims=True)
    return vc * lax.rsqrt(var + LN_EPS)


def _params(*sem):
    return pltpu.CompilerParams(dimension_semantics=sem, vmem_limit_bytes=VMEM_LIMIT)


def _ada_kernel(c_ref, w_ref, b_ref, o_ref):
    s_hi, s_lo = _split_bf16(_silu(c_ref[...]))
    w_hi, w_lo = _split_bf16(w_ref[...])
    o_ref[...] = _dot(s_hi, w_hi) + _dot(s_lo, w_hi) + _dot(s_hi, w_lo) + b_ref[...]


def _ada(c, w_ada, b_ada):
    bsz = c.shape[0]
    width = w_ada.shape[1]
    tn = 1024
    return pl.pallas_call(
        _ada_kernel,
        out_shape=jax.ShapeDtypeStruct((bsz, width), F32),
        grid=(width // tn,),
        in_specs=[pl.BlockSpec((bsz, D), lambda j: (0, 0)),
                  pl.BlockSpec((D, tn), lambda j: (0, j)),
                  pl.BlockSpec((1, tn), lambda j: (0, j))],
        out_specs=pl.BlockSpec((bsz, tn), lambda j: (0, j)),
        compiler_params=_params("arbitrary"),
        name="ada",
    )(c, w_ada, b_ada.reshape(1, width))


def _inproj_kernel(x_ref, mod_ref, w_ref, ws_ref, wst_ref, h_ref, sm_ref, smt_ref, u_scr):
    @pl.when(pl.program_id(1) == 0)
    def _():
        mod = mod_ref[0]
        u = _layer_norm(x_ref[...]) * (1.0 + mod[1:2]) + mod[0:1]
        ub = u.astype(BF16)
        u_scr[...] = ub
        sm_ref[...] = _dot(ub, ws_ref[...])
        smt_ref[...] = _dot_nt(wst_ref[...], ub)

    h_ref[...] = _dot_nt(u_scr[...], w_ref[...]).astype(BF16)


def _inproj(x2, mod3, w_main, w_small, w_small_t, seq):
    n = x2.shape[0]
    tm = min(1024, seq)
    tn = 1024
    tiles_per_batch = seq // tm
    return pl.pallas_call(
        _inproj_kernel,
        out_shape=(jax.ShapeDtypeStruct((n, H_WIDTH), BF16),
                   jax.ShapeDtypeStruct((n, LANES), F32),
                   jax.ShapeDtypeStruct((2 * GDN_HEADS, n), F32)),
        grid=(n // tm, H_WIDTH // tn),
        in_specs=[pl.BlockSpec((tm, D), lambda i, j: (i, 0)),
                  pl.BlockSpec((1, 6, D), lambda i, j: (i // tiles_per_batch, 0, 0)),
                  pl.BlockSpec((tn, D), lambda i, j: (j, 0)),
                  pl.BlockSpec((D, LANES), lambda i, j: (0, 0)),
                  pl.BlockSpec((2 * GDN_HEADS, D), lambda i, j: (0, 0))],
        out_specs=(pl.BlockSpec((tm, tn), lambda i, j: (i, j)),
                   pl.BlockSpec((tm, LANES), lambda i, j: (i, 0)),
                   pl.BlockSpec((2 * GDN_HEADS, tm), lambda i, j: (0, i))),
        scratch_shapes=[pltpu.VMEM((tm, D), BF16)],
        compiler_params=_params("arbitrary", "arbitrary"),
        name="inproj",
    )(x2, mod3, w_main, w_small, w_small_t)


SWA_TQ = 512
SWA_PREV = 128
SWA_BAND = 192


def _swa_kernel(sink_ref, q_ref, kp_ref, vp_ref, kc_ref, vc_ref, o_ref):
    i = pl.program_id(1)
    kwin = jnp.concatenate([kp_ref[...], kc_ref[...]], axis=0)
    vwin = jnp.concatenate([vp_ref[...], vc_ref[...]], axis=0)
    lo_lane = lax.broadcasted_iota(I32, (1, LANES), 1) < HEAD_DIM
    row_top = lax.broadcasted_iota(I32, (2 * CHUNK, 1), 0) < CHUNK
    key_iota = lax.broadcasted_iota(I32, (1, SWA_BAND), 1)
    zero = jnp.zeros((), BF16)
    for kv in range(KV_HEADS):
        k2 = kwin[:, kv * LANES:(kv + 1) * LANES]
        v2 = vwin[:, kv * LANES:(kv + 1) * LANES]
        k_lo = jnp.where(lo_lane, k2, zero)
        k_hi = jnp.where(lo_lane, zero, k2)
        v_lo = jnp.where(lo_lane, v2, zero)
        v_hi = jnp.where(lo_lane, zero, v2)
        sinks = (jnp.where(row_top, sink_ref[kv * 4 + 0], sink_ref[kv * 4 + 2]),
                 jnp.where(row_top, sink_ref[kv * 4 + 1], sink_ref[kv * 4 + 3]))
        base = kv * 2 * LANES
        chunks = range(SWA_TQ // CHUNK)
        items = [(c, par) for c in chunks for par in range(2)]
        ql = [jnp.concatenate([q_ref[c * CHUNK:(c + 1) * CHUNK, base:base + LANES],
                               q_ref[c * CHUNK:(c + 1) * CHUNK, base + LANES:base + 2 * LANES]], axis=0)
              for c in chunks]
        scores = [_dot_nt(ql[c], (k_lo, k_hi)[par][c * CHUNK:c * CHUNK + SWA_BAND]) * (HEAD_DIM ** -0.5)
                  for c, par in items]
        scores = [jnp.where((i * SWA_TQ - SWA_PREV + c * CHUNK + key_iota) >= 0, s, NEG_INF)
                  for (c, par), s in zip(items, scores)]
        top = [jnp.maximum(jnp.max(s, axis=-1, keepdims=True), sinks[par]) for (c, par), s in zip(items, scores)]
        probs = [jnp.exp(s - m) for s, m in zip(scores, top)]
        den = [jnp.sum(p, axis=-1, keepdims=True) + jnp.exp(sinks[par] - m)
               for (c, par), p, m in zip(items, probs, top)]
        outs = [_dot(p.astype(BF16), (v_lo, v_hi)[par][c * CHUNK:c * CHUNK + SWA_BAND]) / d
                for (c, par), p, d in zip(items, probs, den)]
        for c in chunks:
            rows = slice(c * CHUNK, (c + 1) * CHUNK)
            acc = outs[2 * c] + outs[2 * c + 1]
            o_ref[rows, base:base + LANES] = acc[0:CHUNK].astype(BF16)
            o_ref[rows, base + LANES:base + 2 * LANES] = acc[CHUNK:2 * CHUNK].astype(BF16)


def _swa(hcat, sinks, bsz, seq):
    n = hcat.shape[0]
    tq = SWA_TQ
    nq = seq // tq
    kvw = 2 * SWA_KV_W

    def prev_map(b, i, s):
        return (b * (seq // SWA_PREV) + jnp.maximum(i * (tq // SWA_PREV) - 1, 0), COL_K2 // kvw)

    def prev_map_v(b, i, s):
        return (b * (seq // SWA_PREV) + jnp.maximum(i * (tq // SWA_PREV) - 1, 0), COL_V2 // kvw)

    return pl.pallas_call(
        _swa_kernel,
        out_shape=jax.ShapeDtypeStruct((n, SWA_Q_W), BF16),
        grid_spec=pltpu.PrefetchScalarGridSpec(
            num_scalar_prefetch=1,
            grid=(bsz, nq),
            in_specs=[pl.BlockSpec((tq, SWA_Q_W), lambda b, i, s: (b * nq + i, COL_QA // SWA_Q_W)),
                      pl.BlockSpec((SWA_PREV, kvw), prev_map),
                      pl.BlockSpec((SWA_PREV, kvw), prev_map_v),
                      pl.BlockSpec((tq, kvw), lambda b, i, s: (b * nq + i, COL_K2 // kvw)),
                      pl.BlockSpec((tq, kvw), lambda b, i, s: (b * nq + i, COL_V2 // kvw))],
            out_specs=pl.BlockSpec((tq, SWA_Q_W), lambda b, i, s: (b * nq + i, 0))),
        compiler_params=_params("arbitrary", "arbitrary"),
        name="swa",
    )(sinks, hcat, hcat, hcat, hcat, hcat)


GDN_PREV = 16
GDN_CPB = 8
GDN_ROWS = GDN_CPB * CHUNK


def _mm(p, q):
    return _dot(p.astype(BF16), q.astype(BF16))


def _gdn_kernel(qc_ref, kc_ref, vc_ref, z_ref, qp_ref, kp_ref, vp_ref, cw_ref, sm_ref, smt_ref,
                prow_ref, pcol_ref, nw_ref, o_ref, state, xbuf):
    n = pl.program_id(1)
    heads = range(GDN_HEADS)

    @pl.when(n == 0)
    def _():
        state[...] = jnp.zeros_like(state)

    keep = jnp.where(n == 0, 0.0, 1.0)
    for s, (cur, prev) in enumerate(((qc_ref, qp_ref), (kc_ref, kp_ref), (vc_ref, vp_ref))):
        cols = slice(s * GDN_W, (s + 1) * GDN_W)
        xbuf[0:GDN_PREV, cols] = prev[...].astype(F32) * keep
        xbuf[GDN_PREV:GDN_PREV + GDN_ROWS, cols] = cur[...].astype(F32)
    conv = None
    for tap in range(CONV_WIDTH):
        start = GDN_PREV - (CONV_WIDTH - 1) + tap
        term = xbuf[start:start + GDN_ROWS, :] * cw_ref[tap:tap + 1, :]
        conv = term if conv is None else conv + term
    qkv = _silu(conv)

    ri = lax.broadcasted_iota(I32, (CHUNK, CHUNK), 0)
    ci = lax.broadcasted_iota(I32, (CHUNK, CHUNK), 1)
    incl = ri >= ci
    strict = ri > ci
    eye = jnp.where(ri == ci, 1.0, 0.0).astype(F32)
    blk8 = (ri // 8) == (ci // 8)
    blk16 = (ri // 16) == (ci // 16)
    blk32 = (ri // 32) == (ci // 32)
    levels = (blk16 & ~blk8, blk32 & ~blk16, ~blk32)
    tri_lo = jnp.where(incl, 1.0, 0.0).astype(BF16)
    tri_up = jnp.where(ri <= ci, 1.0, 0.0).astype(BF16)

    staged = []

    def advance(c, s_old):
        rows = slice(c * CHUNK, (c + 1) * CHUNK)
        u_c, wq, qk, k_dec, cdec = staged[c]
        ws = [_dot(wq[h], s_old[h].astype(BF16)) for h in heads]
        vb = [(u_c[h] - ws[h][0:CHUNK]).astype(BF16) for h in heads]
        o_c = [ws[h][CHUNK:2 * CHUNK] + _dot(qk[h], vb[h]) for h in heads]
        s_new = [cdec[h] * s_old[h] + _dot_tn(k_dec[h], vb[h]) for h in heads]
        for h in heads:
            lanes = slice(h * GDN_DIM, (h + 1) * GDN_DIM)
            o_n = o_c[h] * lax.rsqrt(jnp.mean(o_c[h] * o_c[h], axis=-1, keepdims=True) + RMS_EPS)
            o_n = o_n * nw_ref[...] * _silu(z_ref[rows, lanes].astype(F32))
            o_ref[rows, lanes] = o_n.astype(BF16)
        return s_new

    s_cur = [state[h] for h in heads]
    for c in range(GDN_CPB):
        rows = slice(c * CHUNK, (c + 1) * CHUNK)
        sm = sm_ref[rows, :]
        beta_all = jax.nn.sigmoid(sm)
        g_all = -jnp.exp(prow_ref[0:1, :]) * _softplus(sm + prow_ref[1:2, :])
        g_hi, g_lo = _split_bf16(g_all)
        cum_all = _dot(tri_lo, g_hi) + _dot(tri_lo, g_lo)
        ecum_all = jnp.exp(cum_all)
        cum_last = cum_all[CHUNK - 1:CHUNK, :]
        kscale_all = jnp.exp(cum_last - cum_all)
        cdec_all = jnp.exp(cum_last)
        smt = smt_ref[c]
        g_row = -jnp.exp(pcol_ref[:, 0:1]) * _softplus(smt[GDN_HEADS:2 * GDN_HEADS, :] + pcol_ref[:, 1:2])
        gr_hi, gr_lo = _split_bf16(g_row)
        cum_row = _dot(gr_hi, tri_up) + _dot(gr_lo, tri_up)

        def col(arr, lane):
            return arr[:, lane:lane + 1]

        qs, ks, vs = [], [], []
        for h in heads:
            qh = qkv[rows, h * GDN_DIM:(h + 1) * GDN_DIM]
            kh = qkv[rows, GDN_W + h * GDN_DIM:GDN_W + (h + 1) * GDN_DIM]
            qs.append(qh * lax.rsqrt(jnp.sum(qh * qh, axis=-1, keepdims=True) + RMS_EPS) * (GDN_DIM ** -0.5))
            ks.append(kh * lax.rsqrt(jnp.sum(kh * kh, axis=-1, keepdims=True) + RMS_EPS))
            vs.append(qkv[rows, 2 * GDN_W + h * GDN_DIM:2 * GDN_W + (h + 1) * GDN_DIM])
        beta = [col(beta_all, h) for h in heads]
        ecum = [col(ecum_all, GDN_HEADS + h) for h in heads]
        dec = []
        for h in heads:
            diff = col(cum_all, GDN_HEADS + h) - cum_row[h:h + 1, :]
            dec.append(jnp.where(incl, jnp.exp(jnp.where(incl, diff, 0.0)), 0.0))
        kb = [ks[h].astype(BF16) for h in heads]
        qkk = [_dot_nt(jnp.concatenate([qs[h].astype(BF16), kb[h]], axis=0), kb[h]) for h in heads]
        qk = [(qkk[h][0:CHUNK] * dec[h]).astype(BF16) for h in heads]
        a_mat = [jnp.where(strict, beta[h] * qkk[h][CHUNK:2 * CHUNK] * dec[h], 0.0) for h in heads]

        a8 = [jnp.where(blk8, a_mat[h], 0.0) for h in heads]
        a8_2 = [_mm(a8[h], a8[h]) for h in heads]
        a8_4 = [_mm(a8_2[h], a8_2[h]) for h in heads]
        t = [_mm(eye - a8[h], eye + a8_2[h]) for h in heads]
        t = [_mm(t[h], eye + a8_4[h]) for h in heads]
        for level in levels:
            inner = [_mm(jnp.where(level, a_mat[h], 0.0), t[h]) for h in heads]
            t = [t[h] - _mm(t[h], inner[h]) for h in heads]

        uw = [_mm(t[h], jnp.concatenate([vs[h] * beta[h], ks[h] * (beta[h] * ecum[h])], axis=1))
              for h in heads]
        wq = [jnp.concatenate([uw[h][:, GDN_DIM:2 * GDN_DIM], qs[h] * ecum[h]], axis=0).astype(BF16)
              for h in heads]
        k_dec = [(ks[h] * col(kscale_all, GDN_HEADS + h)).astype(BF16) for h in heads]
        cdec = [col(cdec_all, GDN_HEADS + h) for h in heads]
        staged.append(([uw[h][:, 0:GDN_DIM] for h in heads], wq, qk, k_dec, cdec))

    for c in range(GDN_CPB):
        s_cur = advance(c, s_cur)
    for h in heads:
        state[h] = s_cur[h]


def _gdn(hcat, sm, smt3, conv_w, prow, pcol, norm_w, bsz, seq):
    n = hcat.shape[0]
    steps = seq // GDN_ROWS
    pb = GDN_ROWS // GDN_PREV

    def cur(col):
        return pl.BlockSpec((GDN_ROWS, GDN_W), lambda b, c: (b * steps + c, col // GDN_W))

    def prev(col):
        return pl.BlockSpec((GDN_PREV, GDN_W),
                            lambda b, c: (b * steps * pb + jnp.maximum(c * pb - 1, 0), col // GDN_W))

    return pl.pallas_call(
        _gdn_kernel,
        out_shape=jax.ShapeDtypeStruct((n, GDN_W), BF16),
        grid=(bsz, steps),
        in_specs=[cur(COL_QB), cur(COL_KB), cur(COL_VB), cur(COL_ZB),
                  prev(COL_QB), prev(COL_KB), prev(COL_VB),
                  pl.BlockSpec((CONV_WIDTH, 3 * GDN_W), lambda b, c: (0, 0)),
                  pl.BlockSpec((GDN_ROWS, LANES), lambda b, c: (b * steps + c, 0)),
                  pl.BlockSpec((GDN_CPB, 2 * GDN_HEADS, CHUNK), lambda b, c: (b * steps + c, 0, 0)),
                  pl.BlockSpec((2, LANES), lambda b, c: (0, 0)),
                  pl.BlockSpec((GDN_HEADS, LANES), lambda b, c: (0, 0)),
                  pl.BlockSpec((1, GDN_DIM), lambda b, c: (0, 0))],
        out_specs=pl.BlockSpec((GDN_ROWS, GDN_W), lambda b, c: (b * steps + c, 0)),
        scratch_shapes=[pltpu.VMEM((GDN_HEADS, GDN_DIM, GDN_DIM), F32),
                        pltpu.VMEM((GDN_PREV + GDN_ROWS, 3 * GDN_W), F32)],
        compiler_params=_params("arbitrary", "arbitrary"),
        name="gdn",
    )(hcat, hcat, hcat, hcat, hcat, hcat, hcat, conv_w, sm, smt3, prow, pcol, norm_w)


OUTPROJ_TM = 256


def _outproj_step(i, oa_ref, ob_ref, ga_ref, gb_ref, x_ref, mod_ref, wa_ref, wb_ref, wo_ref,
                  g1_ref, b1_ref, wr_ref, br_ref, x1_ref, u2_ref, ri_ref, rw_ref, cnt_ref, run,
                  merged_scr, mix_w, mix_r):
    tm = OUTPROJ_TM
    mod = mod_ref[0]
    oa = oa_ref[...]
    ob = ob_ref[...]
    lane = lax.broadcasted_iota(I32, (1, LANES), 1)
    big = jnp.int32(LANES)

    def project(cb):
        cs = slice(cb * OUTPROJ_CB, (cb + 1) * OUTPROJ_CB)
        merged = ((jnp.tanh(ga_ref[:, cs].astype(F32)) + 1.0) * _dot(oa, wa_ref[:, cs])
                  + (jnp.tanh(gb_ref[:, cs].astype(F32)) + 1.0) * _dot(ob, wb_ref[:, cs]))
        merged_scr[:, cs] = merged.astype(BF16)

    def out_project(half):
        cs = slice(half * (D // 2), (half + 1) * (D // 2))
        mix_w[:, cs] = _dot(merged_scr[...], wo_ref[:, cs])

    project(0)
    project(1)

    x1 = _layer_norm(ALPHA * x_ref[...] + (0.5 * mod[2:3]) * mix_r[...]) * g1_ref[...] + b1_ref[...]
    x1_ref[...] = x1

    project(2)
    project(3)

    u2 = _layer_norm(x1) * (1.0 + mod[4:5]) + mod[3:4]
    _store_token_rows(u2_ref, u2)
    u_hi, u_lo = _split_bf16(u2)

    out_project(0)

    hi_pass = _dot(u_hi, wr_ref[...])
    logits = (hi_pass[:, 0:LANES] + hi_pass[:, LANES:2 * LANES]
              + _dot(u_lo, wr_ref[:, 0:LANES]) + br_ref[...])
    gmask = lane < N_GROUPS
    lg = jnp.where(gmask, logits, NEG_INF)
    gmax = jnp.max(lg, axis=-1, keepdims=True)
    gidx = jnp.min(jnp.where(lg == gmax, lane, big), axis=-1, keepdims=True)
    p_group = 1.0 / jnp.sum(jnp.exp(lg - gmax), axis=-1, keepdims=True)
    emask = (lane >> 3) == (gidx + ROUTE_EXPERT_LANE // EPG)
    le = jnp.where(emask, logits, NEG_INF)
    m1 = jnp.max(le, axis=-1, keepdims=True)
    i1 = jnp.min(jnp.where(le == m1, lane, big), axis=-1, keepdims=True)
    le2 = jnp.where(lane == i1, NEG_INF, le)
    m2 = jnp.max(le2, axis=-1, keepdims=True)
    i2 = jnp.min(jnp.where(le2 == m2, lane, big), axis=-1, keepdims=True)
    e2_rel = jnp.exp(m2 - m1)
    wgt1 = p_group / (1.0 + e2_rel)
    wgt2 = p_group * e2_rel / (1.0 + e2_rel)
    e1 = i1 - ROUTE_EXPERT_LANE
    e2 = i2 - ROUTE_EXPERT_LANE

    out_project(1)

    hot1 = lane == e1
    hot2 = lane == e2
    onehot = jnp.where(hot1 | hot2, 1.0, 0.0).astype(F32)
    tr = lax.broadcasted_iota(I32, (tm, tm), 0)
    tc = lax.broadcasted_iota(I32, (tm, tm), 1)
    before = jnp.where(tr > tc, 1.0, 0.0).astype(BF16)
    total = run[...] + _dot(before, onehot.astype(BF16))
    r1 = jnp.sum(jnp.where(hot1, total, 0.0), axis=-1, keepdims=True).astype(I32)
    r2 = jnp.sum(jnp.where(hot2, total, 0.0), axis=-1, keepdims=True).astype(I32)
    live = jnp.where(i == 0, 0.0, 1.0)
    run[...] = (run[...] + jnp.sum(onehot, axis=0, keepdims=True)) * live
    cnt_ref[...] = run[...]
    ri_ref[...] = jnp.where(lane == 0, e1, jnp.where(lane == 1, e2, jnp.where(lane == 2, r1, r2)))
    rw_ref[...] = jnp.where(lane == 0, wgt1, wgt2)


def _outproj_kernel(*refs):
    run, merged_scr, mix_a, mix_b = refs[-4:]
    i = pl.program_id(0)

    @pl.when(i == 0)
    def _():
        run[...] = jnp.zeros_like(run)
        mix_b[...] = jnp.zeros_like(mix_b)

    @pl.when(i % 2 == 0)
    def _():
        _outproj_step(i, *refs[:-2], mix_a, mix_b)

    @pl.when(i % 2 == 1)
    def _():
        _outproj_step(i, *refs[:-2], mix_b, mix_a)


def _outproj(oa, ob, hcat, x2, mod3, wa, wb, wo, ln_g, ln_b, w_route, b_route, seq):
    n = x2.shape[0]
    tm = OUTPROJ_TM
    n_tiles = n // tm
    tiles_per_batch = seq // tm
    const = dict(pipeline_mode=pl.Buffered(1))

    def ahead(i):
        return jnp.minimum(i, n_tiles - 1)

    def behind(i):
        return jnp.maximum(i - 1, 0)

    return pl.pallas_call(
        _outproj_kernel,
        out_shape=(jax.ShapeDtypeStruct((n, D), F32),
                   jax.ShapeDtypeStruct((n * ROW_TILES, LANES), F32),
                   jax.ShapeDtypeStruct((n, LANES), I32),
                   jax.ShapeDtypeStruct((n, LANES), F32),
                   jax.ShapeDtypeStruct((1, LANES), F32)),
        grid=(n_tiles + 1,),
        in_specs=[pl.BlockSpec((tm, SWA_Q_W), lambda i: (ahead(i), 0)),
                  pl.BlockSpec((tm, GDN_W), lambda i: (ahead(i), 0)),
                  pl.BlockSpec((tm, D), lambda i: (ahead(i), COL_GA // D)),
                  pl.BlockSpec((tm, D), lambda i: (ahead(i), COL_GB // D)),
                  pl.BlockSpec((tm, D), lambda i: (behind(i), 0)),
                  pl.BlockSpec((1, 6, D), lambda i: (behind(i) // tiles_per_batch, 0, 0)),
                  pl.BlockSpec((SWA_Q_W, D), lambda i: (0, 0), **const),
                  pl.BlockSpec((GDN_W, D), lambda i: (0, 0), **const),
                  pl.BlockSpec((D, D), lambda i: (0, 0), **const),
                  pl.BlockSpec((1, D), lambda i: (0, 0)),
                  pl.BlockSpec((1, D), lambda i: (0, 0)),
                  pl.BlockSpec((D, 2 * LANES), lambda i: (0, 0), **const),
                  pl.BlockSpec((1, LANES), lambda i: (0, 0))],
        out_specs=(pl.BlockSpec((tm, D), lambda i: (behind(i), 0)),
                   pl.BlockSpec((tm * ROW_TILES, LANES), lambda i: (behind(i), 0)),
                   pl.BlockSpec((tm, LANES), lambda i: (behind(i), 0)),
                   pl.BlockSpec((tm, LANES), lambda i: (behind(i), 0)),
                   pl.BlockSpec((1, LANES), lambda i: (0, 0))),
        scratch_shapes=[pltpu.VMEM((1, LANES), F32),
                        pltpu.VMEM((tm, D), BF16),
                        pltpu.VMEM((tm, D), F32),
                        pltpu.VMEM((tm, D), F32)],
        compiler_params=_params("arbitrary"),
        name="outproj",
    )(oa, ob, hcat, hcat, x2, mod3, wa, wb, wo, ln_g, ln_b, w_route, b_route)


DISPATCH_TM = 512


DMA_UNROLL = 8


def _dispatch_kernel(pos_ref, fill_ref, u_ref, xs_ref, zeros, sem, fill_sem):
    base = pl.program_id(0) * (2 * DISPATCH_TM)
    tile_rows = MOE_TILE * ROW_TILES
    n_tiles = xs_ref.shape[0] // tile_rows

    @pl.when(pl.program_id(0) == 0)
    def _():
        zeros[...] = jnp.zeros_like(zeros)

        def fill_copy(t):
            start = pl.multiple_of(t * tile_rows, tile_rows)
            return pltpu.make_async_copy(zeros, xs_ref.at[pl.ds(start, tile_rows), :], fill_sem)

        def fill_start(t, carry):
            @pl.when(fill_ref[t] == 1)
            def _():
                fill_copy(t).start()
            return carry

        def fill_wait(t, carry):
            @pl.when(fill_ref[t] == 1)
            def _():
                fill_copy(t).wait()
            return carry

        lax.fori_loop(0, n_tiles, fill_start, 0)
        lax.fori_loop(0, n_tiles, fill_wait, 0)

    def row_copy(r, k):
        src = pl.multiple_of(r * ROW_TILES, ROW_TILES)
        dst = pl.multiple_of(pos_ref[base + 2 * r + k], ROW_TILES)
        return pltpu.make_async_copy(u_ref.at[pl.ds(src, ROW_TILES), :],
                                     xs_ref.at[pl.ds(dst, ROW_TILES), :], sem)

    def issue(r, carry):
        row_copy(r, 0).start(priority=0)
        row_copy(r, 1).start(priority=1)
        return carry

    def drain(r, carry):
        row_copy(r, 0).wait()
        row_copy(r, 1).wait()
        return carry

    lax.fori_loop(0, DISPATCH_TM, issue, 0, unroll=DMA_UNROLL)
    lax.fori_loop(0, DISPATCH_TM, drain, 0, unroll=DMA_UNROLL)


def _dispatch(pos, tile_fill, u2, n_pad):
    n = u2.shape[0] // ROW_TILES
    return pl.pallas_call(
        _dispatch_kernel,
        out_shape=jax.ShapeDtypeStruct((n_pad * ROW_TILES, LANES), F32),
        grid_spec=pltpu.PrefetchScalarGridSpec(
            num_scalar_prefetch=2,
            grid=(n // DISPATCH_TM,),
            in_specs=[pl.BlockSpec((DISPATCH_TM * ROW_TILES, LANES), lambda i, p, f: (i, 0))],
            out_specs=pl.BlockSpec(memory_space=pl.ANY),
            scratch_shapes=[pltpu.VMEM((MOE_TILE * ROW_TILES, LANES), F32),
                            pltpu.SemaphoreType.DMA(()),
                            pltpu.SemaphoreType.DMA(())]),
        compiler_params=_params("arbitrary"),
        name="dispatch",
    )(pos, tile_fill, u2)


def _moe_kernel(te_ref, tr_ref, tv_ref, x_ref, wgu_ref, wd_ref, y_ref, wgu_bf, wd_bf):
    i = pl.program_id(0)
    fresh = jnp.logical_or(i == 0, te_ref[i] != te_ref[jnp.maximum(i - 1, 0)])

    @pl.when(jnp.logical_and(fresh, tv_ref[i] == 1))
    def _():
        wgu_bf[...] = wgu_ref[0].astype(BF16)
        wd_bf[...] = wd_ref[0].astype(BF16)

    @pl.when(tv_ref[i] == 1)
    def _():
        x = _load_token_rows(lambda g: x_ref[pl.ds(g, MOE_TILE, stride=ROW_TILES), :]).astype(BF16)
        hid = []
        for fb in range(EXPERT_FF // MOE_FF_BLOCK):
            gate = _dot(x, wgu_bf[:, fb * MOE_FF_BLOCK:(fb + 1) * MOE_FF_BLOCK])
            up = _dot(x, wgu_bf[:, EXPERT_FF + fb * MOE_FF_BLOCK:EXPERT_FF + (fb + 1) * MOE_FF_BLOCK])
            hid.append((_silu(gate) * up).astype(BF16))
        hid = jnp.concatenate(hid, axis=1)
        groups = MOE_OUT_BLOCK // LANES
        for cb in range(D // MOE_OUT_BLOCK):
            y = _dot(hid, wd_bf[:, cb * MOE_OUT_BLOCK:(cb + 1) * MOE_OUT_BLOCK])
            for g in range(groups):
                y_ref[pl.ds(cb * groups + g, MOE_TILE, stride=ROW_TILES), :] = y[:, g * LANES:(g + 1) * LANES]

    @pl.when(tv_ref[i] == 0)
    def _():
        y_ref[...] = jnp.zeros_like(y_ref)


def _moe(tile_expert, tile_row, tile_valid, xs, w_gate_up, w_down):
    tile_rows = MOE_TILE * ROW_TILES
    n_tiles = xs.shape[0] // tile_rows
    return pl.pallas_call(
        _moe_kernel,
        out_shape=jax.ShapeDtypeStruct(xs.shape, F32),
        grid_spec=pltpu.PrefetchScalarGridSpec(
            num_scalar_prefetch=3,
            grid=(n_tiles,),
            in_specs=[pl.BlockSpec((tile_rows, LANES), lambda i, te, tr, tv: (tr[i], 0)),
                      pl.BlockSpec((1, D, 2 * EXPERT_FF), lambda i, te, tr, tv: (te[i], 0, 0)),
                      pl.BlockSpec((1, EXPERT_FF, D), lambda i, te, tr, tv: (te[i], 0, 0))],
            out_specs=pl.BlockSpec((tile_rows, LANES), lambda i, te, tr, tv: (i, 0)),
            scratch_shapes=[pltpu.VMEM((D, 2 * EXPERT_FF), BF16),
                            pltpu.VMEM((EXPERT_FF, D), BF16)]),
        compiler_params=_params("arbitrary"),
        name="moe",
    )(tile_expert, tile_row, tile_valid, xs, w_gate_up, w_down)


COMBINE_TM = 256


def _combine_kernel(pos_ref, ys_ref, rw_ref, x1_ref, mod_ref, g2_ref, b2_ref, o_ref, buf, sems):
    i = pl.program_id(0)
    n_steps = pl.num_programs(0)

    def row_copy(step, slot, r, k):
        src = pl.multiple_of(pos_ref[step * (2 * COMBINE_TM) + 2 * r + k], ROW_TILES)
        dst = pl.multiple_of(r * ROW_TILES, ROW_TILES)
        return pltpu.make_async_copy(ys_ref.at[pl.ds(src, ROW_TILES), :],
                                     buf.at[slot, k, pl.ds(dst, ROW_TILES), :], sems.at[slot])

    def issue(step, slot):
        def body(r, carry):
            row_copy(step, slot, r, 0).start()
            row_copy(step, slot, r, 1).start()
            return carry
        lax.fori_loop(0, COMBINE_TM, body, 0, unroll=DMA_UNROLL)

    def drain(step, slot):
        def body(r, carry):
            row_copy(step, slot, r, 0).wait()
            row_copy(step, slot, r, 1).wait()
            return carry
        lax.fori_loop(0, COMBINE_TM, body, 0, unroll=DMA_UNROLL)

    slot = i % 2

    @pl.when(i == 0)
    def _():
        issue(0, 0)

    @pl.when(i + 1 < n_steps)
    def _():
        issue(i + 1, 1 - slot)

    drain(i, slot)

    rw = rw_ref[...]
    y0 = _load_token_rows(lambda s: buf[slot, 0, pl.ds(s, COMBINE_TM, stride=ROW_TILES), :])
    y1 = _load_token_rows(lambda s: buf[slot, 1, pl.ds(s, COMBINE_TM, stride=ROW_TILES), :])
    ffn = rw[:, 0:1] * y0 + rw[:, 1:2] * y1
    mod = mod_ref[0]
    o_ref[...] = _layer_norm(ALPHA * x1_ref[...] + mod[5:6] * ffn) * g2_ref[...] + b2_ref[...]


def _combine(pos, ys, rw, x1, mod3, ln_g, ln_b, seq):
    n = x1.shape[0]
    tm = COMBINE_TM
    tiles_per_batch = seq // tm
    return pl.pallas_call(
        _combine_kernel,
        out_shape=jax.ShapeDtypeStruct((n, D), F32),
        grid_spec=pltpu.PrefetchScalarGridSpec(
            num_scalar_prefetch=1,
            grid=(n // tm,),
            in_specs=[pl.BlockSpec(memory_space=pl.ANY),
                      pl.BlockSpec((tm, LANES), lambda i, p: (i, 0)),
                      pl.BlockSpec((tm, D), lambda i, p: (i, 0)),
                      pl.BlockSpec((1, 6, D), lambda i, p: (i // tiles_per_batch, 0, 0)),
                      pl.BlockSpec((1, D), lambda i, p: (0, 0)),
                      pl.BlockSpec((1, D), lambda i, p: (0, 0))],
            out_specs=pl.BlockSpec((tm, D), lambda i, p: (i, 0)),
            scratch_shapes=[pltpu.VMEM((2, 2, tm * ROW_TILES, LANES), F32),
                            pltpu.SemaphoreType.DMA((2,))]),
        compiler_params=_params("arbitrary"),
        name="combine",
    )(pos, ys, rw, x1, mod3, ln_g, ln_b)


def _in_weights(w_in):
    sizes = (SWA_Q_W, SWA_KV_W, SWA_KV_W, GDN_W, GDN_W, GDN_W, GDN_W, GDN_HEADS, GDN_HEADS, D, D)
    offs = [0]
    for s in sizes:
        offs.append(offs[-1] + s)
    w_t = w_in.T
    qa, ka, va, qb, kb, vb, zb, bl, al, ga, gb = (w_t[offs[k]:offs[k + 1]] for k in range(len(sizes)))

    def dup(w):
        w4 = w.reshape(KV_HEADS, 1, HEAD_DIM, D)
        return jnp.broadcast_to(w4, (KV_HEADS, 2, HEAD_DIM, D)).reshape(2 * SWA_KV_W, D)

    w_main_t = jnp.concatenate([qa, dup(ka), dup(va), qb, kb, vb, zb, 0.5 * ga, 0.5 * gb], axis=0).astype(BF16)
    small_t = jnp.concatenate([bl, al], axis=0)
    w_small = jnp.pad(small_t.T, ((0, 0), (0, LANES - 2 * GDN_HEADS))).astype(BF16)
    return w_main_t, w_small, small_t.astype(BF16)


def _route_plan(ri, cnt, n_tiles):
    counts = cnt[0, :N_EXPERTS].astype(I32)
    tiles = (counts + MOE_TILE - 1) // MOE_TILE
    tile_end = jnp.cumsum(tiles)
    row_off = (tile_end - tiles) * MOE_TILE
    expert = ri[:, 0:2]
    hit = expert[:, :, None] == jnp.arange(N_EXPERTS, dtype=I32)[None, None, :]
    pos = ((jnp.sum(jnp.where(hit, row_off[None, None, :], 0), axis=-1) + ri[:, 2:4]) * ROW_TILES).reshape(-1)
    used = tile_end[-1]
    t = jnp.arange(n_tiles, dtype=I32)
    t_eff = jnp.minimum(t, used - 1)
    tile_expert = jnp.minimum(jnp.sum(t_eff[:, None] >= tile_end[None, :], axis=1), N_EXPERTS - 1).astype(I32)
    tile_valid = (t < used).astype(I32)
    last_of_expert = jnp.any((t[:, None] + 1 == tile_end[None, :]) & (tiles[None, :] > 0), axis=1)
    tile_fill = jnp.logical_or(last_of_expert, t >= used).astype(I32)
    return pos.astype(I32), tile_expert, t_eff.astype(I32), tile_valid, tile_fill


def kernel(x, c, w_ada, b_ada, w_in, conv_w, swa_sinks, gdn_a_log, gdn_dt_bias, gdn_norm_w, w_proj_a, w_proj_b, w_out, ln1_g, ln1_b, w_router_group, b_router_group, w_router_expert, b_router_expert, w_gate_up, w_down, ln2_g, ln2_b):
    bsz, seq, _ = x.shape
    n = bsz * seq
    nc = seq // CHUNK
    for l in range(DEPTH):
        x2 = x.reshape(n, D)
        mod3 = _ada(c, w_ada[l], b_ada[l]).reshape(bsz, 6, D)
        w_main, w_small, w_small_t = _in_weights(w_in[l])
        hcat, sm, smt = _inproj(x2, mod3, w_main, w_small, w_small_t, seq)
        smt3 = smt.reshape(2 * GDN_HEADS, bsz * nc, CHUNK).transpose(1, 0, 2)

        oa = _swa(hcat, swa_sinks[l], bsz, seq)

        pad_lo = jnp.zeros((GDN_HEADS,), F32)
        prow = jnp.stack([jnp.pad(jnp.concatenate([pad_lo, gdn_a_log[l]]), (0, LANES - 2 * GDN_HEADS)),
                          jnp.pad(jnp.concatenate([pad_lo, gdn_dt_bias[l]]), (0, LANES - 2 * GDN_HEADS))])
        pcol = jnp.pad(jnp.stack([gdn_a_log[l], gdn_dt_bias[l]], axis=1), ((0, 0), (0, LANES - 2)))
        ob = _gdn(hcat, sm, smt3, conv_w[l], prow, pcol, gdn_norm_w[l].reshape(1, GDN_DIM), bsz, seq)

        def route_lanes(group_part, expert_part):
            rows = group_part.shape[0]
            gap = jnp.zeros((rows, ROUTE_EXPERT_LANE - N_GROUPS), F32)
            tail = jnp.zeros((rows, LANES - ROUTE_EXPERT_LANE - N_EXPERTS), F32)
            return jnp.concatenate([group_part, gap, expert_part, tail], axis=1)

        w_route = jnp.concatenate(_split_bf16(route_lanes(w_router_group[l], w_router_expert[l])), axis=1)
        b_route = route_lanes(b_router_group[l].reshape(1, N_GROUPS), b_router_expert[l].reshape(1, N_EXPERTS))
        x1, u2, ri, rw, cnt = _outproj(
            oa, ob, hcat, x2, mod3, w_proj_a[l].astype(BF16), w_proj_b[l].astype(BF16),
            w_out[l].astype(BF16), ln1_g[l].reshape(1, D), ln1_b[l].reshape(1, D), w_route, b_route, seq)

        n_tiles = (2 * n) // MOE_TILE + N_EXPERTS
        pos, tile_expert, tile_row, tile_valid, tile_fill = _route_plan(ri, cnt, n_tiles)
        xs = _dispatch(pos, tile_fill, u2, n_tiles * MOE_TILE)
        ys = _moe(tile_expert, tile_row, tile_valid, xs, w_gate_up[l], w_down[l])
        x2 = _combine(pos, ys, rw, x1, mod3, ln2_g[l].reshape(1, D), ln2_b[l].reshape(1, D), seq)
        x = x2.reshape(bsz, seq, D)
    return x
```

```python
import functools

import jax
import jax.numpy as jnp
from jax import lax
from jax.experimental import pallas as pl
from jax.experimental.pallas import tpu as pltpu

F32 = jnp.float32
BF16 = jnp.bfloat16
I32 = jnp.int32

D = 2048
CHUNK = 64
Q_HEADS = 16
KV_HEADS = 4
HEAD_DIM = 64
SWA_Q_W = 1024
SWA_KV_W = 256
GDN_HEADS = 8
GDN_DIM = 128
GDN_W = 1024
CONV_WIDTH = 4
N_GROUPS = 4
EPG = 8
N_EXPERTS = 32
EXPERT_FF = 512
DEPTH = 1
ALPHA = (2 * DEPTH) ** 0.25
LN_EPS = 1e-5
RMS_EPS = 1e-6
NEG_INF = -1e30

SPLIT_FACTOR = 65537.0
LANES = 128
VMEM_LIMIT = 56 * 1024 * 1024

COL_QA = 0
COL_K2 = 1024
COL_V2 = 1536
COL_QB = 2048
COL_KB = 3072
COL_VB = 4096
COL_ZB = 5120
COL_GA = 6144
COL_GB = 8192
H_WIDTH = 10240

ROUTE_EXPERT_LANE = 32
MOE_TILE = 256
MOE_FF_BLOCK = 256
MOE_OUT_BLOCK = 512
OUTPROJ_CB = 512
ROW_TILES = D // LANES


def _store_token_rows(ref, val):
    for g in range(ROW_TILES):
        ref[pl.ds(g, val.shape[0], stride=ROW_TILES), :] = val[:, g * LANES:(g + 1) * LANES]


def _load_token_rows(row_group):
    return jnp.concatenate([row_group(g) for g in range(ROW_TILES)], axis=1)


def _dot(a, b):
    return jnp.dot(a, b, preferred_element_type=F32)


def _dot_nt(a, b):
    return lax.dot_general(a, b, (((1,), (1,)), ((), ())), preferred_element_type=F32)


def _dot_tn(a, b):
    return lax.dot_general(a, b, (((0,), (0,)), ((), ())), preferred_element_type=F32)


def _split_bf16(v):
    c = v * SPLIT_FACTOR
    hi = c - (c - v)
    return hi.astype(BF16), (v - hi).astype(BF16)


def _silu(v):
    return v * jax.nn.sigmoid(v)


def _softplus(v):
    return jnp.maximum(v, 0.0) + jnp.log(1.0 + jnp.exp(-jnp.abs(v)))


def _layer_norm(v):
    mu = jnp.mean(v, axis=-1, keepdims=True)
    vc = v - mu
    var = jnp.mean(vc * vc, axis=-1, keepdims=True)
    return vc * lax.rsqrt(var + LN_EPS)


def _params(*sem):
    return pltpu.CompilerParams(dimension_semantics=sem, vmem_limit_bytes=VMEM_LIMIT)


def _ada_kernel(c_ref, w_ref, b_ref, o_ref):
    s_hi, s_lo = _split_bf16(_silu(c_ref[...]))
    w_hi, w_lo = _split_bf16(w_ref[...])
    o_ref[...] = _dot(s_hi, w_hi) + _dot(s_lo, w_hi) + _dot(s_hi, w_lo) + b_ref[...]


def _ada(c, w_ada, b_ada):
    bsz = c.shape[0]
    width = w_ada.shape[1]
    tn = 1024
    return pl.pallas_call(
        _ada_kernel,
        out_shape=jax.ShapeDtypeStruct((bsz, width), F32),
        grid=(width // tn,),
        in_specs=[pl.BlockSpec((bsz, D), lambda j: (0, 0)),
                  pl.BlockSpec((D, tn), lambda j: (0, j)),
                  pl.BlockSpec((1, tn), lambda j: (0, j))],
        out_specs=pl.BlockSpec((bsz, tn), lambda j: (0, j)),
        compiler_params=_params("arbitrary"),
        name="ada",
    )(c, w_ada, b_ada.reshape(1, width))


def _inproj_kernel(x_ref, mod_ref, w_ref, ws_ref, wst_ref, h_ref, sm_ref, smt_ref, u_scr):
    @pl.when(pl.program_id(1) == 0)
    def _():
        mod = mod_ref[0]
        u = _layer_norm(x_ref[...]) * (1.0 + mod[1:2]) + mod[0:1]
        ub = u.astype(BF16)
        u_scr[...] = ub
        sm_ref[...] = _dot(ub, ws_ref[...])
        smt_ref[...] = _dot_nt(wst_ref[...], ub)

    h_ref[...] = _dot_nt(u_scr[...], w_ref[...]).astype(BF16)


def _inproj(x2, mod3, w_main, w_small, w_small_t, seq):
    n = x2.shape[0]
    tm = min(1024, seq)
    tn = 1024
    tiles_per_batch = seq // tm
    return pl.pallas_call(
        _inproj_kernel,
        out_shape=(jax.ShapeDtypeStruct((n, H_WIDTH), BF16),
                   jax.ShapeDtypeStruct((n, LANES), F32),
                   jax.ShapeDtypeStruct((2 * GDN_HEADS, n), F32)),
        grid=(n // tm, H_WIDTH // tn),
        in_specs=[pl.BlockSpec((tm, D), lambda i, j: (i, 0)),
                  pl.BlockSpec((1, 6, D), lambda i, j: (i // tiles_per_batch, 0, 0)),
                  pl.BlockSpec((tn, D), lambda i, j: (j, 0)),
                  pl.BlockSpec((D, LANES), lambda i, j: (0, 0)),
                  pl.BlockSpec((2 * GDN_HEADS, D), lambda i, j: (0, 0))],
        out_specs=(pl.BlockSpec((tm, tn), lambda i, j: (i, j)),
                   pl.BlockSpec((tm, LANES), lambda i, j: (i, 0)),
                   pl.BlockSpec((2 * GDN_HEADS, tm), lambda i, j: (0, i))),
        scratch_shapes=[pltpu.VMEM((tm, D), BF16)],
        compiler_params=_params("arbitrary", "arbitrary"),
        name="inproj",
    )(x2, mod3, w_main, w_small, w_small_t)


SWA_TQ = 512
SWA_PREV = 128
SWA_BAND = 192


def _swa_kernel(sink_ref, q_ref, kp_ref, vp_ref, kc_ref, vc_ref, o_ref):
    i = pl.program_id(1)
    kwin = jnp.concatenate([kp_ref[...], kc_ref[...]], axis=0)
    vwin = jnp.concatenate([vp_ref[...], vc_ref[...]], axis=0)
    lo_lane = lax.broadcasted_iota(I32, (1, LANES), 1) < HEAD_DIM
    row_top = lax.broadcasted_iota(I32, (2 * CHUNK, 1), 0) < CHUNK
    key_iota = lax.broadcasted_iota(I32, (1, SWA_BAND), 1)
    zero = jnp.zeros((), BF16)
    for kv in range(KV_HEADS):
        k2 = kwin[:, kv * LANES:(kv + 1) * LANES]
        v2 = vwin[:, kv * LANES:(kv + 1) * LANES]
        k_lo = jnp.where(lo_lane, k2, zero)
        k_hi = jnp.where(lo_lane, zero, k2)
        v_lo = jnp.where(lo_lane, v2, zero)
        v_hi = jnp.where(lo_lane, zero, v2)
        sinks = (jnp.where(row_top, sink_ref[kv * 4 + 0], sink_ref[kv * 4 + 2]),
                 jnp.where(row_top, sink_ref[kv * 4 + 1], sink_ref[kv * 4 + 3]))
        base = kv * 2 * LANES
        chunks = range(SWA_TQ // CHUNK)
        items = [(c, par) for c in chunks for par in range(2)]
        ql = [jnp.concatenate([q_ref[c * CHUNK:(c + 1) * CHUNK, base:base + LANES],
                               q_ref[c * CHUNK:(c + 1) * CHUNK, base + LANES:base + 2 * LANES]], axis=0)
              for c in chunks]
        scores = [_dot_nt(ql[c], (k_lo, k_hi)[par][c * CHUNK:c * CHUNK + SWA_BAND]) * (HEAD_DIM ** -0.5)
                  for c, par in items]
        scores = [jnp.where((i * SWA_TQ - SWA_PREV + c * CHUNK + key_iota) >= 0, s, NEG_INF)
                  for (c, par), s in zip(items, scores)]
        top = [jnp.maximum(jnp.max(s, axis=-1, keepdims=True), sinks[par]) for (c, par), s in zip(items, scores)]
        probs = [jnp.exp(s - m) for s, m in zip(scores, top)]
        den = [jnp.sum(p, axis=-1, keepdims=True) + jnp.exp(sinks[par] - m)
               for (c, par), p, m in zip(items, probs, top)]
        outs = [_dot(p.astype(BF16), (v_lo, v_hi)[par][c * CHUNK:c * CHUNK + SWA_BAND]) / d
                for (c, par), p, d in zip(items, probs, den)]
        for c in chunks:
            rows = slice(c * CHUNK, (c + 1) * CHUNK)
            acc = outs[2 * c] + outs[2 * c + 1]
            o_ref[rows, base:base + LANES] = acc[0:CHUNK].astype(BF16)
            o_ref[rows, base + LANES:base + 2 * LANES] = acc[CHUNK:2 * CHUNK].astype(BF16)


def _swa(hcat, sinks, bsz, seq):
    n = hcat.shape[0]
    tq = SWA_TQ
    nq = seq // tq
    kvw = 2 * SWA_KV_W

    def prev_map(b, i, s):
        return (b * (seq // SWA_PREV) + jnp.maximum(i * (tq // SWA_PREV) - 1, 0), COL_K2 // kvw)

    def prev_map_v(b, i, s):
        return (b * (seq // SWA_PREV) + jnp.maximum(i * (tq // SWA_PREV) - 1, 0), COL_V2 // kvw)

    return pl.pallas_call(
        _swa_kernel,
        out_shape=jax.ShapeDtypeStruct((n, SWA_Q_W), BF16),
        grid_spec=pltpu.PrefetchScalarGridSpec(
            num_scalar_prefetch=1,
            grid=(bsz, nq),
            in_specs=[pl.BlockSpec((tq, SWA_Q_W), lambda b, i, s: (b * nq + i, COL_QA // SWA_Q_W)),
                      pl.BlockSpec((SWA_PREV, kvw), prev_map),
                      pl.BlockSpec((SWA_PREV, kvw), prev_map_v),
                      pl.BlockSpec((tq, kvw), lambda b, i, s: (b * nq + i, COL_K2 // kvw)),
                      pl.BlockSpec((tq, kvw), lambda b, i, s: (b * nq + i, COL_V2 // kvw))],
            out_specs=pl.BlockSpec((tq, SWA_Q_W), lambda b, i, s: (b * nq + i, 0))),
        compiler_params=_params("arbitrary", "arbitrary"),
        name="swa",
    )(sinks, hcat, hcat, hcat, hcat, hcat)


GDN_PREV = 16
GDN_CPB = 8
GDN_ROWS = GDN_CPB * CHUNK


def _mm(p, q):
    return _dot(p.astype(BF16), q.astype(BF16))


def _gdn_kernel(qc_ref, kc_ref, vc_ref, z_ref, qp_ref, kp_ref, vp_ref, cw_ref, sm_ref, smt_ref,
                prow_ref, pcol_ref, nw_ref, o_ref, state, xbuf):
    n = pl.program_id(1)
    heads = range(GDN_HEADS)

    @pl.when(n == 0)
    def _():
        state[...] = jnp.zeros_like(state)

    keep = jnp.where(n == 0, 0.0, 1.0)
    for s, (cur, prev) in enumerate(((qc_ref, qp_ref), (kc_ref, kp_ref), (vc_ref, vp_ref))):
        cols = slice(s * GDN_W, (s + 1) * GDN_W)
        xbuf[0:GDN_PREV, cols] = prev[...].astype(F32) * keep
        xbuf[GDN_PREV:GDN_PREV + GDN_ROWS, cols] = cur[...].astype(F32)
    conv = None
    for tap in range(CONV_WIDTH):
        start = GDN_PREV - (CONV_WIDTH - 1) + tap
        term = xbuf[start:start + GDN_ROWS, :] * cw_ref[tap:tap + 1, :]
        conv = term if conv is None else conv + term
    qkv = _silu(conv)

    ri = lax.broadcasted_iota(I32, (CHUNK, CHUNK), 0)
    ci = lax.broadcasted_iota(I32, (CHUNK, CHUNK), 1)
    incl = ri >= ci
    strict = ri > ci
    eye = jnp.where(ri == ci, 1.0, 0.0).astype(F32)
    blk8 = (ri // 8) == (ci // 8)
    blk16 = (ri // 16) == (ci // 16)
    blk32 = (ri // 32) == (ci // 32)
    levels = (blk16 & ~blk8, blk32 & ~blk16, ~blk32)
    tri_lo = jnp.where(incl, 1.0, 0.0).astype(BF16)
    tri_up = jnp.where(ri <= ci, 1.0, 0.0).astype(BF16)

    staged = []

    def advance(c, s_old):
        rows = slice(c * CHUNK, (c + 1) * CHUNK)
        u_c, wq, qk, k_dec, cdec = staged[c]
        ws = [_dot(wq[h], s_old[h].astype(BF16)) for h in heads]
        vb = [(u_c[h] - ws[h][0:CHUNK]).astype(BF16) for h in heads]
        o_c = [ws[h][CHUNK:2 * CHUNK] + _dot(qk[h], vb[h]) for h in heads]
        s_new = [cdec[h] * s_old[h] + _dot_tn(k_dec[h], vb[h]) for h in heads]
        for h in heads:
            lanes = slice(h * GDN_DIM, (h + 1) * GDN_DIM)
            o_n = o_c[h] * lax.rsqrt(jnp.mean(o_c[h] * o_c[h], axis=-1, keepdims=True) + RMS_EPS)
            o_n = o_n * nw_ref[...] * _silu(z_ref[rows, lanes].astype(F32))
            o_ref[rows, lanes] = o_n.astype(BF16)
        return s_new

    s_cur = [state[h] for h in heads]
    for c in range(GDN_CPB):
        rows = slice(c * CHUNK, (c + 1) * CHUNK)
        sm = sm_ref[rows, :]
        beta_all = jax.nn.sigmoid(sm)
        g_all = -jnp.exp(prow_ref[0:1, :]) * _softplus(sm + prow_ref[1:2, :])
        g_hi, g_lo = _split_bf16(g_all)
        cum_all = _dot(tri_lo, g_hi) + _dot(tri_lo, g_lo)
        ecum_all = jnp.exp(cum_all)
        cum_last = cum_all[CHUNK - 1:CHUNK, :]
        kscale_all = jnp.exp(cum_last - cum_all)
        cdec_all = jnp.exp(cum_last)
        smt = smt_ref[c]
        g_row = -jnp.exp(pcol_ref[:, 0:1]) * _softplus(smt[GDN_HEADS:2 * GDN_HEADS, :] + pcol_ref[:, 1:2])
        gr_hi, gr_lo = _split_bf16(g_row)
        cum_row = _dot(gr_hi, tri_up) + _dot(gr_lo, tri_up)

        def col(arr, lane):
            return arr[:, lane:lane + 1]

        qs, ks, vs = [], [], []
        for h in heads:
            qh = qkv[rows, h * GDN_DIM:(h + 1) * GDN_DIM]
            kh = qkv[rows, GDN_W + h * GDN_DIM:GDN_W + (h + 1) * GDN_DIM]
            qs.append(qh * lax.rsqrt(jnp.sum(qh * qh, axis=-1, keepdims=True) + RMS_EPS) * (GDN_DIM ** -0.5))
            ks.append(kh * lax.rsqrt(jnp.sum(kh * kh, axis=-1, keepdims=True) + RMS_EPS))
            vs.append(qkv[rows, 2 * GDN_W + h * GDN_DIM:2 * GDN_W + (h + 1) * GDN_DIM])
        beta = [col(beta_all, h) for h in heads]
        ecum = [col(ecum_all, GDN_HEADS + h) for h in heads]
        dec = []
        for h in heads:
            diff = col(cum_all, GDN_HEADS + h) - cum_row[h:h + 1, :]
            dec.append(jnp.where(incl, jnp.exp(jnp.where(incl, diff, 0.0)), 0.0))
        kb = [ks[h].astype(BF16) for h in heads]
        qkk = [_dot_nt(jnp.concatenate([qs[h].astype(BF16), kb[h]], axis=0), kb[h]) for h in heads]
        qk = [(qkk[h][0:CHUNK] * dec[h]).astype(BF16) for h in heads]
        a_mat = [jnp.where(strict, beta[h] * qkk[h][CHUNK:2 * CHUNK] * dec[h], 0.0) for h in heads]

        a8 = [jnp.where(blk8, a_mat[h], 0.0) for h in heads]
        a8_2 = [_mm(a8[h], a8[h]) for h in heads]
        a8_4 = [_mm(a8_2[h], a8_2[h]) for h in heads]
        t = [_mm(eye - a8[h], eye + a8_2[h]) for h in heads]
        t = [_mm(t[h], eye + a8_4[h]) for h in heads]
        for level in levels:
            inner = [_mm(jnp.where(level, a_mat[h], 0.0), t[h]) for h in heads]
            t = [t[h] - _mm(t[h], inner[h]) for h in heads]

        uw = [_mm(t[h], jnp.concatenate([vs[h] * beta[h], ks[h] * (beta[h] * ecum[h])], axis=1))
              for h in heads]
        wq = [jnp.concatenate([uw[h][:, GDN_DIM:2 * GDN_DIM], qs[h] * ecum[h]], axis=0).astype(BF16)
              for h in heads]
        k_dec = [(ks[h] * col(kscale_all, GDN_HEADS + h)).astype(BF16) for h in heads]
        cdec = [col(cdec_all, GDN_HEADS + h) for h in heads]
        staged.append(([uw[h][:, 0:GDN_DIM] for h in heads], wq, qk, k_dec, cdec))

    for c in range(GDN_CPB):
        s_cur = advance(c, s_cur)
    for h in heads:
        state[h] = s_cur[h]


def _gdn(hcat, sm, smt3, conv_w, prow, pcol, norm_w, bsz, seq):
    n = hcat.shape[0]
    steps = seq // GDN_ROWS
    pb = GDN_ROWS // GDN_PREV

    def cur(col):
        return pl.BlockSpec((GDN_ROWS, GDN_W), lambda b, c: (b * steps + c, col // GDN_W))

    def prev(col):
        return pl.BlockSpec((GDN_PREV, GDN_W),
                            lambda b, c: (b * steps * pb + jnp.maximum(c * pb - 1, 0), col // GDN_W))

    return pl.pallas_call(
        _gdn_kernel,
        out_shape=jax.ShapeDtypeStruct((n, GDN_W), BF16),
        grid=(bsz, steps),
        in_specs=[cur(COL_QB), cur(COL_KB), cur(COL_VB), cur(COL_ZB),
                  prev(COL_QB), prev(COL_KB), prev(COL_VB),
                  pl.BlockSpec((CONV_WIDTH, 3 * GDN_W), lambda b, c: (0, 0)),
                  pl.BlockSpec((GDN_ROWS, LANES), lambda b, c: (b * steps + c, 0)),
                  pl.BlockSpec((GDN_CPB, 2 * GDN_HEADS, CHUNK), lambda b, c: (b * steps + c, 0, 0)),
                  pl.BlockSpec((2, LANES), lambda b, c: (0, 0)),
                  pl.BlockSpec((GDN_HEADS, LANES), lambda b, c: (0, 0)),
                  pl.BlockSpec((1, GDN_DIM), lambda b, c: (0, 0))],
        out_specs=pl.BlockSpec((GDN_ROWS, GDN_W), lambda b, c: (b * steps + c, 0)),
        scratch_shapes=[pltpu.VMEM((GDN_HEADS, GDN_DIM, GDN_DIM), F32),
                        pltpu.VMEM((GDN_PREV + GDN_ROWS, 3 * GDN_W), F32)],
        compiler_params=_params("arbitrary", "arbitrary"),
        name="gdn",
    )(hcat, hcat, hcat, hcat, hcat, hcat, hcat, conv_w, sm, smt3, prow, pcol, norm_w)


OUTPROJ_TM = 256


def _outproj_step(i, oa_ref, ob_ref, ga_ref, gb_ref, x_ref, mod_ref, wa_ref, wb_ref, wo_ref,
                  g1_ref, b1_ref, wr_ref, br_ref, x1_ref, u2_ref, ri_ref, rw_ref, cnt_ref, run,
                  merged_scr, mix_w, mix_r):
    tm = OUTPROJ_TM
    mod = mod_ref[0]
    oa = oa_ref[...]
    ob = ob_ref[...]
    lane = lax.broadcasted_iota(I32, (1, LANES), 1)
    big = jnp.int32(LANES)

    def project(cb):
        cs = slice(cb * OUTPROJ_CB, (cb + 1) * OUTPROJ_CB)
        merged = ((jnp.tanh(ga_ref[:, cs].astype(F32)) + 1.0) * _dot(oa, wa_ref[:, cs])
                  + (jnp.tanh(gb_ref[:, cs].astype(F32)) + 1.0) * _dot(ob, wb_ref[:, cs]))
        merged_scr[:, cs] = merged.astype(BF16)

    def out_project(half):
        cs = slice(half * (D // 2), (half + 1) * (D // 2))
        mix_w[:, cs] = _dot(merged_scr[...], wo_ref[:, cs])

    project(0)
    project(1)

    x1 = _layer_norm(ALPHA * x_ref[...] + (0.5 * mod[2:3]) * mix_r[...]) * g1_ref[...] + b1_ref[...]
    x1_ref[...] = x1

    project(2)
    project(3)

    u2 = _layer_norm(x1) * (1.0 + mod[4:5]) + mod[3:4]
    _store_token_rows(u2_ref, u2)
    u_hi, u_lo = _split_bf16(u2)

    out_project(0)

    hi_pass = _dot(u_hi, wr_ref[...])
    logits = (hi_pass[:, 0:LANES] + hi_pass[:, LANES:2 * LANES]
              + _dot(u_lo, wr_ref[:, 0:LANES]) + br_ref[...])
    gmask = lane < N_GROUPS
    lg = jnp.where(gmask, logits, NEG_INF)
    gmax = jnp.max(lg, axis=-1, keepdims=True)
    gidx = jnp.min(jnp.where(lg == gmax, lane, big), axis=-1, keepdims=True)
    p_group = 1.0 / jnp.sum(jnp.exp(lg - gmax), axis=-1, keepdims=True)
    emask = (lane >> 3) == (gidx + ROUTE_EXPERT_LANE // EPG)
    le = jnp.where(emask, logits, NEG_INF)
    m1 = jnp.max(le, axis=-1, keepdims=True)
    i1 = jnp.min(jnp.where(le == m1, lane, big), axis=-1, keepdims=True)
    le2 = jnp.where(lane == i1, NEG_INF, le)
    m2 = jnp.max(le2, axis=-1, keepdims=True)
    i2 = jnp.min(jnp.where(le2 == m2, lane, big), axis=-1, keepdims=True)
    e2_rel = jnp.exp(m2 - m1)
    wgt1 = p_group / (1.0 + e2_rel)
    wgt2 = p_group * e2_rel / (1.0 + e2_rel)
    e1 = i1 - ROUTE_EXPERT_LANE
    e2 = i2 - ROUTE_EXPERT_LANE

    out_project(1)

    hot1 = lane == e1
    hot2 = lane == e2
    onehot = jnp.where(hot1 | hot2, 1.0, 0.0).astype(F32)
    tr = lax.broadcasted_iota(I32, (tm, tm), 0)
    tc = lax.broadcasted_iota(I32, (tm, tm), 1)
    before = jnp.where(tr > tc, 1.0, 0.0).astype(BF16)
    total = run[...] + _dot(before, onehot.astype(BF16))
    r1 = jnp.sum(jnp.where(hot1, total, 0.0), axis=-1, keepdims=True).astype(I32)
    r2 = jnp.sum(jnp.where(hot2, total, 0.0), axis=-1, keepdims=True).astype(I32)
    live = jnp.where(i == 0, 0.0, 1.0)
    run[...] = (run[...] + jnp.sum(onehot, axis=0, keepdims=True)) * live
    cnt_ref[...] = run[...]
    ri_ref[...] = jnp.where(lane == 0, e1, jnp.where(lane == 1, e2, jnp.where(lane == 2, r1, r2)))
    rw_ref[...] = jnp.where(lane == 0, wgt1, wgt2)


def _outproj_kernel(*refs):
    run, merged_scr, mix_a, mix_b = refs[-4:]
    i = pl.program_id(0)

    @pl.when(i == 0)
    def _():
        run[...] = jnp.zeros_like(run)
        mix_b[...] = jnp.zeros_like(mix_b)

    @pl.when(i % 2 == 0)
    def _():
        _outproj_step(i, *refs[:-2], mix_a, mix_b)

    @pl.when(i % 2 == 1)
    def _():
        _outproj_step(i, *refs[:-2], mix_b, mix_a)


def _outproj(oa, ob, hcat, x2, mod3, wa, wb, wo, ln_g, ln_b, w_route, b_route, seq):
    n = x2.shape[0]
    tm = OUTPROJ_TM
    n_tiles = n // tm
    tiles_per_batch = seq // tm
    const = dict(pipeline_mode=pl.Buffered(1))

    def ahead(i):
        return jnp.minimum(i, n_tiles - 1)

    def behind(i):
        return jnp.maximum(i - 1, 0)

    return pl.pallas_call(
        _outproj_kernel,
        out_shape=(jax.ShapeDtypeStruct((n, D), F32),
                   jax.ShapeDtypeStruct((n * ROW_TILES, LANES), F32),
                   jax.ShapeDtypeStruct((n, LANES), I32),
                   jax.ShapeDtypeStruct((n, LANES), F32),
                   jax.ShapeDtypeStruct((1, LANES), F32)),
        grid=(n_tiles + 1,),
        in_specs=[pl.BlockSpec((tm, SWA_Q_W), lambda i: (ahead(i), 0)),
                  pl.BlockSpec((tm, GDN_W), lambda i: (ahead(i), 0)),
                  pl.BlockSpec((tm, D), lambda i: (ahead(i), COL_GA // D)),
                  pl.BlockSpec((tm, D), lambda i: (ahead(i), COL_GB // D)),
                  pl.BlockSpec((tm, D), lambda i: (behind(i), 0)),
                  pl.BlockSpec((1, 6, D), lambda i: (behind(i) // tiles_per_batch, 0, 0)),
                  pl.BlockSpec((SWA_Q_W, D), lambda i: (0, 0), **const),
                  pl.BlockSpec((GDN_W, D), lambda i: (0, 0), **const),
                  pl.BlockSpec((D, D), lambda i: (0, 0), **const),
                  pl.BlockSpec((1, D), lambda i: (0, 0)),
                  pl.BlockSpec((1, D), lambda i: (0, 0)),
                  pl.BlockSpec((D, 2 * LANES), lambda i: (0, 0), **const),
                  pl.BlockSpec((1, LANES), lambda i: (0, 0))],
        out_specs=(pl.BlockSpec((tm, D), lambda i: (behind(i), 0)),
                   pl.BlockSpec((tm * ROW_TILES, LANES), lambda i: (behind(i), 0)),
                   pl.BlockSpec((tm, LANES), lambda i: (behind(i), 0)),
                   pl.BlockSpec((tm, LANES), lambda i: (behind(i), 0)),
                   pl.BlockSpec((1, LANES), lambda i: (0, 0))),
        scratch_shapes=[pltpu.VMEM((1, LANES), F32),
                        pltpu.VMEM((tm, D), BF16),
                        pltpu.VMEM((tm, D), F32),
                        pltpu.VMEM((tm, D), F32)],
        compiler_params=_params("arbitrary"),
        name="outproj",
    )(oa, ob, hcat, hcat, x2, mod3, wa, wb, wo, ln_g, ln_b, w_route, b_route)


DISPATCH_TM = 512


DMA_UNROLL = 8


def _dispatch_kernel(pos_ref, fill_ref, u_ref, xs_ref, zeros, sem, fill_sem):
    base = pl.program_id(0) * (2 * DISPATCH_TM)
    tile_rows = MOE_TILE * ROW_TILES
    n_tiles = xs_ref.shape[0] // tile_rows

    @pl.when(pl.program_id(0) == 0)
    def _():
        zeros[...] = jnp.zeros_like(zeros)

        def fill_copy(t):
            start = pl.multiple_of(t * tile_rows, tile_rows)
            return pltpu.make_async_copy(zeros, xs_ref.at[pl.ds(start, tile_rows), :], fill_sem)

        def fill_start(t, carry):
            @pl.when(fill_ref[t] == 1)
            def _():
                fill_copy(t).start()
            return carry

        def fill_wait(t, carry):
            @pl.when(fill_ref[t] == 1)
            def _():
                fill_copy(t).wait()
            return carry

        lax.fori_loop(0, n_tiles, fill_start, 0)
        lax.fori_loop(0, n_tiles, fill_wait, 0)

    def row_copy(r, k):
        src = pl.multiple_of(r * ROW_TILES, ROW_TILES)
        dst = pl.multiple_of(pos_ref[base + 2 * r + k], ROW_TILES)
        return pltpu.make_async_copy(u_ref.at[pl.ds(src, ROW_TILES), :],
                                     xs_ref.at[pl.ds(dst, ROW_TILES), :], sem)

    def issue(r, carry):
        row_copy(r, 0).start(priority=0)
        row_copy(r, 1).start(priority=1)
        return carry

    lax.fori_loop(0, DISPATCH_TM, issue, 0, unroll=DMA_UNROLL)
    for _ in range(2):
        pltpu.make_async_copy(u_ref, xs_ref.at[pl.ds(0, DISPATCH_TM * ROW_TILES), :], sem).wait()


def _dispatch(pos, tile_fill, u2, n_pad):
    n = u2.shape[0] // ROW_TILES
    return pl.pallas_call(
        _dispatch_kernel,
        out_shape=jax.ShapeDtypeStruct((n_pad * ROW_TILES, LANES), F32),
        grid_spec=pltpu.PrefetchScalarGridSpec(
            num_scalar_prefetch=2,
            grid=(n // DISPATCH_TM,),
            in_specs=[pl.BlockSpec((DISPATCH_TM * ROW_TILES, LANES), lambda i, p, f: (i, 0))],
            out_specs=pl.BlockSpec(memory_space=pl.ANY),
            scratch_shapes=[pltpu.VMEM((MOE_TILE * ROW_TILES, LANES), F32),
                            pltpu.SemaphoreType.DMA(()),
                            pltpu.SemaphoreType.DMA(())]),
        compiler_params=_params("arbitrary"),
        name="dispatch",
    )(pos, tile_fill, u2)


def _moe_kernel(te_ref, tr_ref, tv_ref, x_ref, wgu_ref, wd_ref, y_ref, wgu_bf, wd_bf):
    i = pl.program_id(0)
    fresh = jnp.logical_or(i == 0, te_ref[i] != te_ref[jnp.maximum(i - 1, 0)])

    @pl.when(jnp.logical_and(fresh, tv_ref[i] == 1))
    def _():
        wgu_bf[...] = wgu_ref[0].astype(BF16)
        wd_bf[...] = wd_ref[0].astype(BF16)

    @pl.when(tv_ref[i] == 1)
    def _():
        x = _load_token_rows(lambda g: x_ref[pl.ds(g, MOE_TILE, stride=ROW_TILES), :]).astype(BF16)
        hid = []
        for fb in range(EXPERT_FF // MOE_FF_BLOCK):
            gate = _dot(x, wgu_bf[:, fb * MOE_FF_BLOCK:(fb + 1) * MOE_FF_BLOCK])
            up = _dot(x, wgu_bf[:, EXPERT_FF + fb * MOE_FF_BLOCK:EXPERT_FF + (fb + 1) * MOE_FF_BLOCK])
            hid.append((_silu(gate) * up).astype(BF16))
        hid = jnp.concatenate(hid, axis=1)
        groups = MOE_OUT_BLOCK // LANES
        for cb in range(D // MOE_OUT_BLOCK):
            y = _dot(hid, wd_bf[:, cb * MOE_OUT_BLOCK:(cb + 1) * MOE_OUT_BLOCK])
            for g in range(groups):
                y_ref[pl.ds(cb * groups + g, MOE_TILE, stride=ROW_TILES), :] = y[:, g * LANES:(g + 1) * LANES]

    @pl.when(tv_ref[i] == 0)
    def _():
        y_ref[...] = jnp.zeros_like(y_ref)


def _moe(tile_expert, tile_row, tile_valid, xs, w_gate_up, w_down):
    tile_rows = MOE_TILE * ROW_TILES
    n_tiles = xs.shape[0] // tile_rows
    return pl.pallas_call(
        _moe_kernel,
        out_shape=jax.ShapeDtypeStruct(xs.shape, F32),
        grid_spec=pltpu.PrefetchScalarGridSpec(
            num_scalar_prefetch=3,
            grid=(n_tiles,),
            in_specs=[pl.BlockSpec((tile_rows, LANES), lambda i, te, tr, tv: (tr[i], 0)),
                      pl.BlockSpec((1, D, 2 * EXPERT_FF), lambda i, te, tr, tv: (te[i], 0, 0)),
                      pl.BlockSpec((1, EXPERT_FF, D), lambda i, te, tr, tv: (te[i], 0, 0))],
            out_specs=pl.BlockSpec((tile_rows, LANES), lambda i, te, tr, tv: (i, 0)),
            scratch_shapes=[pltpu.VMEM((D, 2 * EXPERT_FF), BF16),
                            pltpu.VMEM((EXPERT_FF, D), BF16)]),
        compiler_params=_params("arbitrary"),
        name="moe",
    )(tile_expert, tile_row, tile_valid, xs, w_gate_up, w_down)


COMBINE_TM = 256


def _combine_kernel(pos_ref, ys_ref, rw_ref, x1_ref, mod_ref, g2_ref, b2_ref, o_ref, buf, sems):
    i = pl.program_id(0)
    n_steps = pl.num_programs(0)

    def row_copy(step, slot, r, k):
        src = pl.multiple_of(pos_ref[step * (2 * COMBINE_TM) + 2 * r + k], ROW_TILES)
        dst = pl.multiple_of(r * ROW_TILES, ROW_TILES)
        return pltpu.make_async_copy(ys_ref.at[pl.ds(src, ROW_TILES), :],
                                     buf.at[slot, k, pl.ds(dst, ROW_TILES), :], sems.at[slot])

    def issue(step, slot):
        def body(r, carry):
            row_copy(step, slot, r, 0).start()
            row_copy(step, slot, r, 1).start()
            return carry
        lax.fori_loop(0, COMBINE_TM, body, 0, unroll=DMA_UNROLL)

    def drain(step, slot):
        for k in range(2):
            pltpu.make_async_copy(ys_ref.at[pl.ds(0, COMBINE_TM * ROW_TILES), :],
                                  buf.at[slot, k], sems.at[slot]).wait()

    slot = i % 2

    @pl.when(i == 0)
    def _():
        issue(0, 0)

    @pl.when(i + 1 < n_steps)
    def _():
        issue(i + 1, 1 - slot)

    drain(i, slot)

    rw = rw_ref[...]
    y0 = _load_token_rows(lambda s: buf[slot, 0, pl.ds(s, COMBINE_TM, stride=ROW_TILES), :])
    y1 = _load_token_rows(lambda s: buf[slot, 1, pl.ds(s, COMBINE_TM, stride=ROW_TILES), :])
    ffn = rw[:, 0:1] * y0 + rw[:, 1:2] * y1
    mod = mod_ref[0]
    o_ref[...] = _layer_norm(ALPHA * x1_ref[...] + mod[5:6] * ffn) * g2_ref[...] + b2_ref[...]


def _combine(pos, ys, rw, x1, mod3, ln_g, ln_b, seq):
    n = x1.shape[0]
    tm = COMBINE_TM
    tiles_per_batch = seq // tm
    return pl.pallas_call(
        _combine_kernel,
        out_shape=jax.ShapeDtypeStruct((n, D), F32),
        grid_spec=pltpu.PrefetchScalarGridSpec(
            num_scalar_prefetch=1,
            grid=(n // tm,),
            in_specs=[pl.BlockSpec(memory_space=pl.ANY),
                      pl.BlockSpec((tm, LANES), lambda i, p: (i, 0)),
                      pl.BlockSpec((tm, D), lambda i, p: (i, 0)),
                      pl.BlockSpec((1, 6, D), lambda i, p: (i // tiles_per_batch, 0, 0)),
                      pl.BlockSpec((1, D), lambda i, p: (0, 0)),
                      pl.BlockSpec((1, D), lambda i, p: (0, 0))],
            out_specs=pl.BlockSpec((tm, D), lambda i, p: (i, 0)),
            scratch_shapes=[pltpu.VMEM((2, 2, tm * ROW_TILES, LANES), F32),
                            pltpu.SemaphoreType.DMA((2,))]),
        compiler_params=_params("arbitrary"),
        name="combine",
    )(pos, ys, rw, x1, mod3, ln_g, ln_b)


def _in_weights(w_in):
    sizes = (SWA_Q_W, SWA_KV_W, SWA_KV_W, GDN_W, GDN_W, GDN_W, GDN_W, GDN_HEADS, GDN_HEADS, D, D)
    offs = [0]
    for s in sizes:
        offs.append(offs[-1] + s)
    w_t = w_in.T
    qa, ka, va, qb, kb, vb, zb, bl, al, ga, gb = (w_t[offs[k]:offs[k + 1]] for k in range(len(sizes)))

    def dup(w):
        w4 = w.reshape(KV_HEADS, 1, HEAD_DIM, D)
        return jnp.broadcast_to(w4, (KV_HEADS, 2, HEAD_DIM, D)).reshape(2 * SWA_KV_W, D)

    w_main_t = jnp.concatenate([qa, dup(ka), dup(va), qb, kb, vb, zb, 0.5 * ga, 0.5 * gb], axis=0).astype(BF16)
    small_t = jnp.concatenate([bl, al], axis=0)
    w_small = jnp.pad(small_t.T, ((0, 0), (0, LANES - 2 * GDN_HEADS))).astype(BF16)
    return w_main_t, w_small, small_t.astype(BF16)


def _route_plan(ri, cnt, n_tiles):
    counts = cnt[0, :N_EXPERTS].astype(I32)
    tiles = (counts + MOE_TILE - 1) // MOE_TILE
    tile_end = jnp.cumsum(tiles)
    row_off = (tile_end - tiles) * MOE_TILE
    expert = ri[:, 0:2]
    hit = expert[:, :, None] == jnp.arange(N_EXPERTS, dtype=I32)[None, None, :]
    pos = ((jnp.sum(jnp.where(hit, row_off[None, None, :], 0), axis=-1) + ri[:, 2:4]) * ROW_TILES).reshape(-1)
    used = tile_end[-1]
    t = jnp.arange(n_tiles, dtype=I32)
    t_eff = jnp.minimum(t, used - 1)
    tile_expert = jnp.minimum(jnp.sum(t_eff[:, None] >= tile_end[None, :], axis=1), N_EXPERTS - 1).astype(I32)
    tile_valid = (t < used).astype(I32)
    last_of_expert = jnp.any((t[:, None] + 1 == tile_end[None, :]) & (tiles[None, :] > 0), axis=1)
    tile_fill = jnp.logical_or(last_of_expert, t >= used).astype(I32)
    return pos.astype(I32), tile_expert, t_eff.astype(I32), tile_valid, tile_fill


def kernel(x, c, w_ada, b_ada, w_in, conv_w, swa_sinks, gdn_a_log, gdn_dt_bias, gdn_norm_w, w_proj_a, w_proj_b, w_out, ln1_g, ln1_b, w_router_group, b_router_group, w_router_expert, b_router_expert, w_gate_up, w_down, ln2_g, ln2_b):
    bsz, seq, _ = x.shape
    n = bsz * seq
    nc = seq // CHUNK
    for l in range(DEPTH):
        x2 = x.reshape(n, D)
        mod3 = _ada(c, w_ada[l], b_ada[l]).reshape(bsz, 6, D)
        w_main, w_small, w_small_t = _in_weights(w_in[l])
        hcat, sm, smt = _inproj(x2, mod3, w_main, w_small, w_small_t, seq)
        smt3 = smt.reshape(2 * GDN_HEADS, bsz * nc, CHUNK).transpose(1, 0, 2)

        oa = _swa(hcat, swa_sinks[l], bsz, seq)

        pad_lo = jnp.zeros((GDN_HEADS,), F32)
        prow = jnp.stack([jnp.pad(jnp.concatenate([pad_lo, gdn_a_log[l]]), (0, LANES - 2 * GDN_HEADS)),
                          jnp.pad(jnp.concatenate([pad_lo, gdn_dt_bias[l]]), (0, LANES - 2 * GDN_HEADS))])
        pcol = jnp.pad(jnp.stack([gdn_a_log[l], gdn_dt_bias[l]], axis=1), ((0, 0), (0, LANES - 2)))
        ob = _gdn(hcat, sm, smt3, conv_w[l], prow, pcol, gdn_norm_w[l].reshape(1, GDN_DIM), bsz, seq)

        def route_lanes(group_part, expert_part):
            rows = group_part.shape[0]
            gap = jnp.zeros((rows, ROUTE_EXPERT_LANE - N_GROUPS), F32)
            tail = jnp.zeros((rows, LANES - ROUTE_EXPERT_LANE - N_EXPERTS), F32)
            return jnp.concatenate([group_part, gap, expert_part, tail], axis=1)

        w_route = jnp.concatenate(_split_bf16(route_lanes(w_router_group[l], w_router_expert[l])), axis=1)
        b_route = route_lanes(b_router_group[l].reshape(1, N_GROUPS), b_router_expert[l].reshape(1, N_EXPERTS))
        x1, u2, ri, rw, cnt = _outproj(
            oa, ob, hcat, x2, mod3, w_proj_a[l].astype(BF16), w_proj_b[l].astype(BF16),
            w_out[l].astype(BF16), ln1_g[l].reshape(1, D), ln1_b[l].reshape(1, D), w_route, b_route, seq)

        n_tiles = (2 * n) // MOE_TILE + N_EXPERTS
        pos, tile_expert, tile_row, tile_valid, tile_fill = _route_plan(ri, cnt, n_tiles)
        xs = _dispatch(pos, tile_fill, u2, n_tiles * MOE_TILE)
        ys = _moe(tile_expert, tile_row, tile_valid, xs, w_gate_up[l], w_down[l])
        x2 = _combine(pos, ys, rw, x1, mod3, ln2_g[l].reshape(1, D), ln2_b[l].reshape(1, D), seq)
        x = x2.reshape(bsz, seq, D)
    return x
```
